```python
import math
import jax, jax.numpy as jnp
from jax import lax
import numpy as np

D_MODEL = 1024
BATCH = 2
SEQ = 8192
DEPTH = 4
DEC_BATCH = 32
DEC_SEQ = 1
PAST_LEN = 8192
PAGE_SIZE = 128

HEAD_DIM = 64
MOBA_HEADS = 8
MOBA_BLOCK = 256
MOBA_TOPK = 3
MOBA_QBLK = 32
NSA_HEADS = 8
NSA_KV_HEADS = 2
NSA_CMP_LEN = 32
NSA_CMP_STRIDE = 16
NSA_CMP_HIDDEN = 128
NSA_SEL_BLOCK = 64
NSA_SEL_TOPK = 16
NSA_WINDOW = 512
FOX_HEADS = 16
QBLK = 128
PEER_HEADS = 8
PEER_NKEYS = 128
PEER_EXPERTS = PEER_NKEYS * PEER_NKEYS
PEER_TOPK = 16
PEER_DKEY = 256
PEER_TBLK = 128

N_EVEN = (DEPTH + 1) // 2
N_ODD = DEPTH // 2
ALPHA = (2 * DEPTH) ** 0.25
BETA = (8 * DEPTH) ** -0.25
LN_EPS = 1e-5
EVEN_COLS = (MOBA_HEADS * HEAD_DIM,) * 3 + (NSA_HEADS * HEAD_DIM,) + (NSA_KV_HEADS * HEAD_DIM,) * 6 + (NSA_HEADS * 3,)
ODD_COLS = (FOX_HEADS * HEAD_DIM,) * 3 + (FOX_HEADS,)

kernel_name = "hybrid_moba_nsa_fox_peer_step"


def alibi_slopes(n):
    return jnp.asarray(2.0 ** (-8.0 * np.arange(1, n + 1) / n), jnp.float32)


def layer_norm(x, g, b):
    xf = x.astype(jnp.float32)
    xc = xf - xf.mean(-1, keepdims=True)
    var = jnp.mean(xc * xc, -1, keepdims=True)
    return (xc * lax.rsqrt(var + LN_EPS) * g + b).astype(x.dtype)


def masked_softmax(s, mask):
    s = jnp.where(mask, s, -jnp.inf)
    m = jnp.max(s, -1, keepdims=True)
    m = jnp.where(jnp.isfinite(m), m, 0.0)
    e = jnp.exp(s - m)
    return e / jnp.maximum(e.sum(-1, keepdims=True), 1e-30)


def map_queries(fn, blk, qpos, *args):
    nq = qpos.shape[0]
    if nq <= blk or nq % blk:
        return fn(qpos, *args)
    n = nq // blk
    split = [a.reshape(a.shape[0], n, blk, *a.shape[2:]).swapaxes(0, 1) for a in args]
    out = lax.map(lambda xs: fn(xs[0], *xs[1:]), (qpos.reshape(n, blk), *split))
    out = out.swapaxes(0, 1)
    return out.reshape(out.shape[0], nq, *out.shape[3:])


def split_cols(y, sizes):
    return jnp.split(y, np.cumsum(sizes)[:-1].tolist(), axis=-1)


def pad_rows(a, n):
    return jnp.pad(a, [(0, 0), (0, n - a.shape[1])] + [(0, 0)] * (a.ndim - 2))


def gather_pages(pool, page_table, layer):
    g = pool[page_table, layer]
    return g.reshape((g.shape[0], -1) + g.shape[3:])


def moba_mix(q, k_new, v_new, k_past, v_past):
    b, nq, h, dh = q.shape
    past = k_past.shape[1]
    nblk = max(-(-(past + nq) // MOBA_BLOCK), MOBA_TOPK)
    k = pad_rows(jnp.concatenate([k_past, k_new], 1), nblk * MOBA_BLOCK)
    v = pad_rows(jnp.concatenate([v_past, v_new], 1), nblk * MOBA_BLOCK)
    k_blk = k.reshape(b, nblk, MOBA_BLOCK, h, dh).transpose(0, 3, 1, 2, 4)
    v_blk = v.reshape(b, nblk, MOBA_BLOCK, h, dh).transpose(0, 3, 1, 2, 4)
    k_mean = k_blk.astype(jnp.float32).mean(3)
    slopes = alibi_slopes(h)
    scale = dh ** -0.5
    bi = jnp.arange(b)[:, None, None, None]
    hi = jnp.arange(h)[None, None, :, None]
    blk_ids = jnp.arange(nblk)
    offs = jnp.arange(MOBA_BLOCK)

    def block(qpos, qb):
        nqb = qb.shape[1]
        n_past = qpos // MOBA_BLOCK
        gate = jnp.einsum('bqhd,bhnd->bqhn', qb.astype(jnp.float32), k_mean)
        gate = jnp.where((blk_ids[None, :] < n_past[:, None])[None, :, None, :], gate, -jnp.inf)
        _, top = lax.top_k(gate, MOBA_TOPK)
        own = jnp.broadcast_to(n_past[None, :, None, None], (b, nqb, h, 1)).astype(top.dtype)
        idx = jnp.concatenate([top, own], -1)
        slot_ok = jnp.concatenate([jnp.arange(MOBA_TOPK)[None, :] < n_past[:, None],
                                   jnp.ones((nqb, 1), bool)], -1)
        kg = k_blk[bi, hi, idx]
        vg = v_blk[bi, hi, idx]
        dist = qpos[None, :, None, None, None] - (idx[..., None] * MOBA_BLOCK + offs)
        s = (jnp.einsum('bqhd,bqhjkd->bqhjk', qb, kg).astype(jnp.float32) * scale
             - slopes[None, None, :, None, None] * dist)
        mask = (dist >= 0) & slot_ok[None, :, None, :, None]
        p = masked_softmax(s.reshape(b, nqb, h, -1), mask.reshape(b, nqb, h, -1))
        return jnp.einsum('bqhn,bqhnd->bqhd', p.astype(vg.dtype), vg.reshape(b, nqb, h, -1, dh))

    return map_queries(block, MOBA_QBLK, past + jnp.arange(nq), q)


def nsa_compress(k, pos, w1, w2):
    b, n, g, dh = k.shape
    r = NSA_CMP_LEN // NSA_CMP_STRIDE
    kd = k.reshape(b, n // NSA_CMP_STRIDE, NSA_CMP_STRIDE, g, dh)
    nc = n // NSA_CMP_STRIDE - r + 1
    blocks = jnp.concatenate([kd[:, i:i + nc] for i in range(r)], 2) + pos[:, None, :]
    flat = blocks.transpose(0, 1, 3, 2, 4).reshape(b, nc, g, NSA_CMP_LEN * dh)
    return jax.nn.gelu(flat @ w1, approximate=False) @ w2


def nsa_mix(q, gates, kc_new, vc_new, ks_new, vs_new, kw_new, vw_new,
            kc_past, vc_past, ks_past, vs_past, kw_buf, vw_buf, cmp_pos, cmp_w1, cmp_w2):
    b, nq, h, dh = q.shape
    g = kc_new.shape[2]
    r = h // g
    past = kc_past.shape[1]
    n_pad = max(-(-(past + nq) // NSA_SEL_BLOCK), NSA_SEL_TOPK) * NSA_SEL_BLOCK
    kc = pad_rows(jnp.concatenate([kc_past, kc_new], 1), n_pad)
    vc = pad_rows(jnp.concatenate([vc_past, vc_new], 1), n_pad)
    ks = pad_rows(jnp.concatenate([ks_past, ks_new], 1), n_pad)
    vs = pad_rows(jnp.concatenate([vs_past, vs_new], 1), n_pad)
    k_cmp = nsa_compress(kc, cmp_pos[0], cmp_w1[0], cmp_w2[0])
    v_cmp = nsa_compress(vc, cmp_pos[1], cmp_w1[1], cmp_w2[1])
    cmp_start = jnp.arange(k_cmp.shape[1]) * NSA_CMP_STRIDE
    cmp_end = cmp_start + NSA_CMP_LEN - 1
    n_sel = n_pad // NSA_SEL_BLOCK
    sel_start = jnp.arange(n_sel) * NSA_SEL_BLOCK
    cover = ((cmp_start[:, None] < sel_start[None, :] + NSA_SEL_BLOCK)
             & (cmp_start[:, None] + NSA_CMP_LEN > sel_start[None, :])).astype(jnp.float32)
    ks_blk = ks.reshape(b, n_sel, NSA_SEL_BLOCK, g, dh).transpose(0, 3, 1, 2, 4)
    vs_blk = vs.reshape(b, n_sel, NSA_SEL_BLOCK, g, dh).transpose(0, 3, 1, 2, 4)
    win_start = past - kw_buf.shape[1]
    zpad = jnp.zeros((b, NSA_WINDOW, g, dh), kw_new.dtype)
    kw_src = jnp.concatenate([zpad, kw_buf, kw_new], 1)
    vw_src = jnp.concatenate([zpad, vw_buf, vw_new], 1)
    slopes = alibi_slopes(h).reshape(g, r)
    scale = dh ** -0.5
    bi = jnp.arange(b)[:, None, None, None]
    gi = jnp.arange(g)[None, None, :, None]
    sel_ids = jnp.arange(n_sel)
    sel_offs = jnp.arange(NSA_SEL_BLOCK)

    def block(qpos, qb, gb):
        nqb = qb.shape[1]
        qg = qb.reshape(b, nqb, g, r, dh)
        d_c = qpos[:, None] - cmp_end[None, :]
        s_c = (jnp.einsum('bqgrd,bngd->bqgrn', qg, k_cmp).astype(jnp.float32) * scale
               - slopes[:, :, None] * d_c[:, None, None, :])
        p_c = masked_softmax(s_c, (d_c >= 0)[:, None, None, :])
        o_c = jnp.einsum('bqgrn,bngd->bqgrd', p_c.astype(v_cmp.dtype), v_cmp)
        imp = jnp.einsum('bqgrn,nm->bqgm', p_c, cover)
        cur = qpos // NSA_SEL_BLOCK
        causal = sel_ids[None, :] <= cur[:, None]
        forced = ((sel_ids[None, :] == 0) | (sel_ids[None, :] == cur[:, None])
                  | (sel_ids[None, :] == cur[:, None] - 1))
        imp = jnp.where(causal[None, :, None, :],
                        jnp.where(forced[None, :, None, :], jnp.inf, imp), -jnp.inf)
        _, sidx = lax.top_k(imp, NSA_SEL_TOPK)
        slot_ok = jnp.arange(NSA_SEL_TOPK)[None, :] < (cur + 1)[:, None]
        kg = ks_blk[bi, gi, sidx]
        vg = vs_blk[bi, gi, sidx]
        d_s = qpos[None, :, None, None, None] - (sidx[..., None] * NSA_SEL_BLOCK + sel_offs)
        s_s = (jnp.einsum('bqgrd,bqgnkd->bqgrnk', qg, kg).astype(jnp.float32) * scale
               - slopes[None, None, :, :, None, None] * d_s[:, :, :, None])
        m_s = jnp.broadcast_to(((d_s >= 0) & slot_ok[None, :, None, :, None])[:, :, :, None], s_s.shape)
        p_s = masked_softmax(s_s.reshape(b, nqb, g, r, -1), m_s.reshape(b, nqb, g, r, -1))
        o_s = jnp.einsum('bqgrn,bqgnd->bqgrd', p_s.astype(vg.dtype), vg.reshape(b, nqb, g, -1, dh))
        span = NSA_WINDOW + nqb
        kw = lax.dynamic_slice_in_dim(kw_src, qpos[0] - win_start, span, axis=1)
        vw = lax.dynamic_slice_in_dim(vw_src, qpos[0] - win_start, span, axis=1)
        kw_pos = qpos[0] - NSA_WINDOW + jnp.arange(span)
        d_w = qpos[:, None] - kw_pos[None, :]
        m_w = (d_w >= 0) & (d_w <= NSA_WINDOW) & (kw_pos >= win_start)[None, :]
        s_w = (jnp.einsum('bqgrd,bkgd->bqgrk', qg, kw).astype(jnp.float32) * scale
               - slopes[:, :, None] * d_w[:, None, None, :])
        p_w = masked_softmax(s_w, m_w[:, None, None, :])
        o_w = jnp.einsum('bqgrk,bkgd->bqgrd', p_w.astype(vw.dtype), vw)
        gg = gb.reshape(b, nqb, g, r, 3)
        o = gg[..., 0:1] * o_c + gg[..., 1:2] * o_s + gg[..., 2:3] * o_w
        return o.reshape(b, nqb, h, dh)

    out = map_queries(block, QBLK, past + jnp.arange(nq), q, gates)
    kw_all = jnp.concatenate([kw_buf, kw_new], 1)
    vw_all = jnp.concatenate([vw_buf, vw_new], 1)
    keep = min(NSA_WINDOW, kw_all.shape[1])
    new_win = jnp.stack([kw_all[:, kw_all.shape[1] - keep:], vw_all[:, vw_all.shape[1] - keep:]], 2)
    return out, new_win


def fox_mix(q, k_new, v_new, logf_new, k_past, v_past, logf_past):
    b, nq, h, dh = q.shape
    past = k_past.shape[1]
    k = jnp.concatenate([k_past, k_new], 1)
    v = jnp.concatenate([v_past, v_new], 1)
    cum = jnp.cumsum(jnp.concatenate([logf_past.astype(jnp.float32), logf_new.astype(jnp.float32)], 1), axis=1)
    cum_k = cum.swapaxes(1, 2)[:, None]
    kpos = jnp.arange(past + nq)
    scale = dh ** -0.5

    def block(qpos, qb, cq):
        s = jnp.einsum('bqhd,bkhd->bqhk', qb, k).astype(jnp.float32) * scale + cq[..., None] - cum_k
        p = masked_softmax(s, (kpos[None, :] <= qpos[:, None])[None, :, None, :])
        return jnp.einsum('bqhk,bkhd->bqhd', p.astype(v.dtype), v)

    return map_queries(block, QBLK, past + jnp.arange(nq), q, cum[:, past:])


def peer_ffn(x, wq, sub_keys, u, v):
    t, d = x.shape

    def rows(xb):
        n = xb.shape[0]
        q = (xb @ wq).reshape(n, PEER_HEADS, 2, PEER_DKEY // 2)
        s = jnp.einsum('thcd,cnd->thcn', q, sub_keys).astype(jnp.float32)
        s1, i1 = lax.top_k(s[:, :, 0], PEER_TOPK)
        s2, i2 = lax.top_k(s[:, :, 1], PEER_TOPK)
        cand = (s1[..., :, None] + s2[..., None, :]).reshape(n, PEER_HEADS, -1)
        cand_idx = (i1[..., :, None] * PEER_NKEYS + i2[..., None, :]).reshape(n, PEER_HEADS, -1)
        top_s, pos = lax.top_k(cand, PEER_TOPK)
        eidx = jnp.take_along_axis(cand_idx, pos, -1)
        gate = jax.nn.softmax(top_s, -1)
        act = jax.nn.gelu(jnp.einsum('td,thkd->thk', xb, u[eidx]).astype(jnp.float32), approximate=False)
        return jnp.einsum('thk,thkd->td', (gate * act).astype(v.dtype), v[eidx])

    if t > PEER_TBLK and t % PEER_TBLK == 0:
        return lax.map(rows, x.reshape(t // PEER_TBLK, PEER_TBLK, d)).reshape(t, d)
    return rows(x)


def decoder_trunk(x, moba_past, nsa_past, win_past, fox_past, logf_past,
                  w_in_ab, w_out_ab, nsa_cmp_pos, nsa_cmp_w1, nsa_cmp_w2, w_in_c, b_forget, w_out_c,
                  ln_mix_g, ln_mix_b, ln_ffn_g, ln_ffn_b, peer_wq, peer_subkeys, peer_u, peer_v):
    b, n, d = x.shape
    heads = lambda a: a.reshape(b, n, -1, HEAD_DIM)
    moba_rows, nsa_rows, win_rows, fox_rows, logf_rows = [], [], [], [], []
    h = x
    for layer in range(DEPTH):
        i = layer // 2
        if layer % 2 == 0:
            qa, ka, va, qb, kc, vc, ks, vs, kw, vw, gl = [heads(a) if j < 10 else a for j, a in
                                                          enumerate(split_cols(h @ w_in_ab[i], EVEN_COLS))]
            mp = moba_past(i)
            npst = nsa_past(i)
            wp = win_past(i)
            o_a = moba_mix(qa, ka, va, mp[:, :, 0], mp[:, :, 1])
            o_b, win = nsa_mix(qb, jax.nn.sigmoid(gl.reshape(b, n, NSA_HEADS, 3)), kc, vc, ks, vs, kw, vw,
                               npst[:, :, 0], npst[:, :, 1], npst[:, :, 2], npst[:, :, 3],
                               wp[:, :, 0], wp[:, :, 1], nsa_cmp_pos[i], nsa_cmp_w1[i], nsa_cmp_w2[i])
            mixed = jnp.concatenate([o_a.reshape(b, n, -1), o_b.reshape(b, n, -1)], -1) @ w_out_ab[i]
            moba_rows.append(jnp.stack([ka, va], 2))
            nsa_rows.append(jnp.stack([kc, vc, ks, vs], 2))
            win_rows.append(win)
        else:
            qc, kc, vc, fl = split_cols(h @ w_in_c[i], ODD_COLS)
            qc, kc, vc = heads(qc), heads(kc), heads(vc)
            logf = jax.nn.log_sigmoid((fl + b_forget[i]).astype(jnp.float32))
            fp = fox_past(i)
            o_c = fox_mix(qc, kc, vc, logf, fp[:, :, 0], fp[:, :, 1], logf_past(i))
            mixed = o_c.reshape(b, n, -1) @ w_out_c[i]
            fox_rows.append(jnp.stack([kc, vc], 2))
            logf_rows.append(logf)
        h = layer_norm(ALPHA * h + mixed, ln_mix_g[layer], ln_mix_b[layer])
        ffn = peer_ffn(h.reshape(b * n, d), peer_wq[layer], peer_subkeys[layer], peer_u[layer], peer_v[layer])
        h = layer_norm(ALPHA * h + ffn.reshape(b, n, d), ln_ffn_g[layer], ln_ffn_b[layer])
    return (h, jnp.stack(moba_rows, 1), jnp.stack(nsa_rows, 1), jnp.stack(win_rows, 1),
            jnp.stack(fox_rows, 1), jnp.stack(logf_rows, 1))


def setup_inputs(seed: int = 0) -> dict:
    key = jax.random.key(seed)
    ks = jax.random.split(key, 26)
    f32 = jnp.float32
    d, hd = D_MODEL, HEAD_DIM
    n_pages = PAST_LEN // PAGE_SIZE
    n_used = DEC_BATCH * n_pages
    n_pool = n_used + -(-n_used // 4)
    win_rows = min(NSA_WINDOW, PAST_LEN)
    nrm = lambda k, shape, s=1.0: s * jax.random.normal(k, shape, f32)
    even_scale = jnp.concatenate([jnp.full((c,), s, f32) for c, s in
                                  zip(EVEN_COLS, (1.0, 1.0, BETA, 1.0, 1.0, BETA, 1.0, BETA, 1.0, BETA, 1.0))])
    odd_scale = jnp.concatenate([jnp.full((c,), s, f32) for c, s in zip(ODD_COLS, (1.0, 1.0, BETA, 0.5))])
    mix_w = MOBA_HEADS * hd + NSA_HEADS * hd
    return {
        'x_prompt': nrm(ks[0], (BATCH, SEQ, d)),
        'x_sample': nrm(ks[1], (DEC_BATCH, DEC_SEQ, d)),
        'cache_moba_kv': nrm(ks[2], (n_pool, N_EVEN, PAGE_SIZE, 2, MOBA_HEADS, hd)),
        'cache_nsa_kv': nrm(ks[3], (n_pool, N_EVEN, PAGE_SIZE, 4, NSA_KV_HEADS, hd)),
        'state_nsa_win': nrm(ks[4], (DEC_BATCH, N_EVEN, win_rows, 2, NSA_KV_HEADS, hd)),
        'cache_fox_kv': nrm(ks[5], (n_pool, N_ODD, PAGE_SIZE, 2, FOX_HEADS, hd)),
        'cache_fox_logf': jax.nn.log_sigmoid(3.0 + nrm(ks[6], (n_pool, N_ODD, PAGE_SIZE, FOX_HEADS))),
        'page_table': jax.random.permutation(ks[7], n_pool)[:n_used].reshape(DEC_BATCH, n_pages).astype(jnp.int32),
        'w_in_ab': nrm(ks[8], (N_EVEN, d, sum(EVEN_COLS)), d ** -0.5) * even_scale,
        'w_out_ab': nrm(ks[9], (N_EVEN, mix_w, d), BETA * mix_w ** -0.5),
        'nsa_cmp_pos': nrm(ks[10], (N_EVEN, 2, NSA_CMP_LEN, hd), 0.1),
        'nsa_cmp_w1': nrm(ks[11], (N_EVEN, 2, NSA_CMP_LEN * hd, NSA_CMP_HIDDEN), (NSA_CMP_LEN * hd) ** -0.5),
        'nsa_cmp_w2': nrm(ks[12], (N_EVEN, 2, NSA_CMP_HIDDEN, hd), NSA_CMP_HIDDEN ** -0.5),
        'w_in_c': nrm(ks[13], (N_ODD, d, sum(ODD_COLS)), d ** -0.5) * odd_scale,
        'b_forget': 2.0 + 3.0 * jax.random.uniform(ks[14], (N_ODD, FOX_HEADS), f32),
        'w_out_c': nrm(ks[15], (N_ODD, FOX_HEADS * hd, d), BETA * (FOX_HEADS * hd) ** -0.5),
        'ln_mix_g': 1.0 + nrm(ks[16], (DEPTH, d), 0.05),
        'ln_mix_b': nrm(ks[17], (DEPTH, d), 0.02),
        'ln_ffn_g': 1.0 + nrm(ks[18], (DEPTH, d), 0.05),
        'ln_ffn_b': nrm(ks[19], (DEPTH, d), 0.02),
        'peer_wq': nrm(ks[20], (DEPTH, d, PEER_HEADS * PEER_DKEY), d ** -0.5),
        'peer_subkeys': nrm(ks[21], (DEPTH, 2, PEER_NKEYS, PEER_DKEY // 2), (PEER_DKEY // 2) ** -0.5),
        'peer_u': nrm(ks[22], (DEPTH, PEER_EXPERTS, d), d ** -0.5),
        'peer_v': nrm(ks[23], (DEPTH, PEER_EXPERTS, d), BETA * PEER_HEADS ** -0.5),
    }


def reference(x_prompt, x_sample, cache_moba_kv, cache_nsa_kv, state_nsa_win, cache_fox_kv, cache_fox_logf,
              page_table, w_in_ab, w_out_ab, nsa_cmp_pos, nsa_cmp_w1, nsa_cmp_w2, w_in_c, b_forget, w_out_c,
              ln_mix_g, ln_mix_b, ln_ffn_g, ln_ffn_b, peer_wq, peer_subkeys, peer_u, peer_v):
    weights = (w_in_ab, w_out_ab, nsa_cmp_pos, nsa_cmp_w1, nsa_cmp_w2, w_in_c, b_forget, w_out_c,
               ln_mix_g, ln_mix_b, ln_ffn_g, ln_ffn_b, peer_wq, peer_subkeys, peer_u, peer_v)
    bp, dt = x_prompt.shape[0], x_prompt.dtype
    empty = lambda *shape: (lambda i: jnp.zeros((bp, 0) + shape, dt))
    y_p, moba_p, nsa_p, win_p, fox_p, logf_p = decoder_trunk(
        x_prompt, empty(2, MOBA_HEADS, HEAD_DIM), empty(4, NSA_KV_HEADS, HEAD_DIM),
        empty(2, NSA_KV_HEADS, HEAD_DIM), empty(2, FOX_HEADS, HEAD_DIM), empty(FOX_HEADS), *weights)
    paged = lambda pool: (lambda i: gather_pages(pool, page_table, i))
    y_s, moba_s, nsa_s, win_s, fox_s, logf_s = decoder_trunk(
        x_sample, paged(cache_moba_kv), paged(cache_nsa_kv), lambda i: state_nsa_win[:, i],
        paged(cache_fox_kv), paged(cache_fox_logf), *weights)
    return (y_p, y_s, moba_p, moba_s, nsa_p, nsa_s, win_p, win_s, fox_p, fox_s, logf_p, logf_s)
```

```python
import functools
import math

import numpy as np
import jax
import jax.numpy as jnp
from jax import lax
from jax.experimental import pallas as pl
from jax.experimental.pallas import tpu as pltpu

F32 = jnp.float32
BF16 = jnp.bfloat16

D_MODEL = 1024
DEPTH = 4
PAGE_SIZE = 128
HEAD_DIM = 64
LANES = 128
MOBA_HEADS = 8
MOBA_BLOCK = 256
MOBA_TOPK = 3
MOBA_QBLK = 32
NSA_HEADS = 8
NSA_KV_HEADS = 2
NSA_CMP_LEN = 32
NSA_CMP_STRIDE = 16
NSA_CMP_HIDDEN = 128
NSA_SEL_BLOCK = 64
NSA_SEL_TOPK = 16
NSA_WINDOW = 512
FOX_HEADS = 16
QBLK = 128
PEER_HEADS = 8
PEER_NKEYS = 128
PEER_EXPERTS = PEER_NKEYS * PEER_NKEYS
PEER_TOPK = 16
PEER_DKEY = 256
PEER_TBLK = 128
N_EVEN = (DEPTH + 1) // 2
N_ODD = DEPTH // 2
ALPHA = (2 * DEPTH) ** 0.25
LN_EPS = 1e-5
EVEN_COLS = (MOBA_HEADS * HEAD_DIM,) * 3 + (NSA_HEADS * HEAD_DIM,) + (NSA_KV_HEADS * HEAD_DIM,) * 6 + (NSA_HEADS * 3,)
ODD_COLS = (FOX_HEADS * HEAD_DIM,) * 3 + (FOX_HEADS,)
EVEN_OFF = tuple(int(v) for v in np.cumsum((0,) + EVEN_COLS))
ODD_OFF = tuple(int(v) for v in np.cumsum((0,) + ODD_COLS))
SCALE = HEAD_DIM ** -0.5
NEG = -1e30
VMEM_LIMIT = 56 * 1024 * 1024


def _cparams(sem):
    return pltpu.CompilerParams(dimension_semantics=sem, vmem_limit_bytes=VMEM_LIMIT)


def _round_up(n, m):
    return -(-n // m) * m


def _nt_dot(a, b):
    return lax.dot_general(a, b, (((1,), (1,)), ((), ())), preferred_element_type=F32)


def _mm_kernel(x_ref, w_ref, o_ref):
    o_ref[...] = jnp.dot(x_ref[...].astype(BF16), w_ref[...], preferred_element_type=F32)


def matmul(x, w, tm, tn):
    m, k = x.shape
    n = w.shape[1]
    return pl.pallas_call(
        _mm_kernel, grid=(m // tm, n // tn),
        in_specs=[pl.BlockSpec((tm, k), lambda i, j: (i, 0)), pl.BlockSpec((k, tn), lambda i, j: (0, j))],
        out_specs=pl.BlockSpec((tm, tn), lambda i, j: (i, j)),
        out_shape=jax.ShapeDtypeStruct((m, n), F32),
        compiler_params=_cparams(("parallel", "parallel")), name="proj_matmul")(x, w)


def _layer_norm_rows(z, g, b):
    mu = jnp.mean(z, axis=-1, keepdims=True)
    zc = z - mu
    var = jnp.mean(zc * zc, axis=-1, keepdims=True)
    return zc * lax.rsqrt(var + LN_EPS) * g + b


def _out_even_kernel(oa_ref, oc_ref, os_ref, ow_ref, gl_ref, x_ref, w_ref, g_ref, b_ref, y_ref):
    sig = jax.nn.sigmoid(gl_ref[...])
    width = NSA_HEADS * HEAD_DIM
    head_of_lane = lax.broadcasted_iota(jnp.int32, (1, width), 1) // HEAD_DIM

    def expand(j):
        acc = jnp.zeros((sig.shape[0], width), F32)
        for h in range(NSA_HEADS):
            acc = jnp.where(head_of_lane == h, sig[:, 3 * h + j:3 * h + j + 1], acc)
        return acc

    ob = expand(0) * oc_ref[...] + expand(1) * os_ref[...] + expand(2) * ow_ref[...]
    half = MOBA_HEADS * HEAD_DIM
    mixed = (jnp.dot(oa_ref[...].astype(BF16), w_ref[0:half, :], preferred_element_type=F32)
             + jnp.dot(ob.astype(BF16), w_ref[half:, :], preferred_element_type=F32))
    y_ref[...] = _layer_norm_rows(ALPHA * x_ref[...] + mixed, g_ref[...], b_ref[...])


def out_proj_even(oa, oc, os_, ow, proj, x, w, g, b, tm):
    m = x.shape[0]
    hw = MOBA_HEADS * HEAD_DIM
    row = lambda i: (i, 0)
    fixed = lambda i: (0, 0)
    return pl.pallas_call(
        _out_even_kernel, grid=(m // tm,),
        in_specs=[pl.BlockSpec((tm, hw), row)] * 4
        + [pl.BlockSpec((tm, LANES), lambda i: (i, EVEN_OFF[10] // LANES)),
           pl.BlockSpec((tm, D_MODEL), row), pl.BlockSpec((2 * hw, D_MODEL), fixed),
           pl.BlockSpec((1, D_MODEL), fixed), pl.BlockSpec((1, D_MODEL), fixed)],
        out_specs=pl.BlockSpec((tm, D_MODEL), row),
        out_shape=jax.ShapeDtypeStruct((m, D_MODEL), F32),
        compiler_params=_cparams(("parallel",)), name="out_proj_even")(oa, oc, os_, ow, proj, x, w, g, b)


def _out_odd_kernel(o_ref, x_ref, w_ref, g_ref, b_ref, y_ref):
    mixed = jnp.dot(o_ref[...].astype(BF16), w_ref[...], preferred_element_type=F32)
    y_ref[...] = _layer_norm_rows(ALPHA * x_ref[...] + mixed, g_ref[...], b_ref[...])


def out_proj_odd(o, x, w, g, b, tm):
    m = x.shape[0]
    row = lambda i: (i, 0)
    fixed = lambda i: (0, 0)
    return pl.pallas_call(
        _out_odd_kernel, grid=(m // tm,),
        in_specs=[pl.BlockSpec((tm, D_MODEL), row), pl.BlockSpec((tm, D_MODEL), row),
                  pl.BlockSpec((D_MODEL, D_MODEL), fixed),
                  pl.BlockSpec((1, D_MODEL), fixed), pl.BlockSpec((1, D_MODEL), fixed)],
        out_specs=pl.BlockSpec((tm, D_MODEL), row),
        out_shape=jax.ShapeDtypeStruct((m, D_MODEL), F32),
        compiler_params=_cparams(("parallel",)), name="out_proj_odd")(o, x, w, g, b)


def _stack_queries(q_ref, q_scr, rows_cfg, tq):
    lane = lax.broadcasted_iota(jnp.int32, (tq, LANES), 1)
    for r, (qb, qh, kh) in enumerate(rows_cfg):
        x = q_ref[0, :, qb * LANES:(qb + 1) * LANES]
        if qh != kh:
            x = pltpu.roll(x, HEAD_DIM, axis=1)
        x = jnp.where((lane >= kh * HEAD_DIM) & (lane < (kh + 1) * HEAD_DIM), x * SCALE, 0.0)
        q_scr[r * tq:(r + 1) * tq, :] = x.astype(BF16)


def _assemble_heads(vals, rows_cfg, out_cfg, tq):
    lane = lax.broadcasted_iota(jnp.int32, (tq, LANES), 1)
    nblk = max(ob for ob, _ in out_cfg) + 1
    blocks = []
    for j in range(nblk):
        parts = {}
        for r, (ob, oh) in enumerate(out_cfg):
            if ob != j:
                continue
            x = vals[r]
            if rows_cfg[r][2] != oh:
                x = pltpu.roll(x, HEAD_DIM, axis=1)
            parts[oh] = x
        blocks.append(jnp.where(lane < HEAD_DIM, parts[0], parts[1]))
    return blocks


def _pair_cfg():
    return [(0, 0, 0), (0, 1, 1)], [(0, 0), (0, 1)]


def _group_cfg(g):
    rows = [(r // 2, r % 2, g) for r in range(4)]
    outs = [(r // 2, r % 2) for r in range(4)]
    return rows, outs


def _flash_kernel(*refs, mode, R, Rs, tq, tk, W, rows_cfg, out_cfg, nk):
    if mode == "fox":
        q_ref, k_ref, v_ref, cq_ref, ck_ref, o_ref, q_scr, m_scr, l_scr, acc_scr, cq_scr = refs
    elif mode == "sel":
        sl_ref, q_ref, k_ref, v_ref, sel_ref, o_ref, q_scr, m_scr, l_scr, acc_scr = refs
    else:
        sl_ref, q_ref, k_ref, v_ref, o_ref, q_scr, m_scr, l_scr, acc_scr = refs
    qi = pl.program_id(2)
    ki = pl.program_id(3)
    q0 = qi * tq

    @pl.when(ki == 0)
    def _init():
        _stack_queries(q_ref, q_scr, rows_cfg, tq)
        m_scr[...] = jnp.full(m_scr.shape, NEG, F32)
        l_scr[...] = jnp.zeros(l_scr.shape, F32)
        acc_scr[...] = jnp.zeros(acc_scr.shape, F32)
        if mode == "fox":
            eye = (lax.broadcasted_iota(jnp.int32, (tq, tq), 0) == lax.broadcasted_iota(jnp.int32, (tq, tq), 1))
            for r in range(R):
                row = cq_ref[0, 0, r:r + 1, :]
                cq_scr[r * tq:(r + 1) * tq, :] = jnp.sum(jnp.where(eye, row, 0.0), axis=1, keepdims=True)

    if mode == "win":
        kt = qi - 1 + ki
        active = kt >= 0
        k0 = kt * tk
    else:
        active = ki * tk <= q0 + tq - 1
        k0 = ki * tk

    @pl.when(active)
    def _step():
        k = k_ref[0].astype(BF16)
        v = v_ref[0].astype(BF16)
        s = _nt_dot(q_scr[...], k)
        qpos = q0 + lax.broadcasted_iota(jnp.int32, (tq, 1), 0)
        kpos = k0 + lax.broadcasted_iota(jnp.int32, (1, tk), 1)
        if mode == "win":
            d = qpos - kpos
            base_mask = (d >= 0) & (d <= NSA_WINDOW)
        else:
            base_mask = kpos <= qpos
        if mode == "sel":
            jrow = lax.broadcasted_iota(jnp.int32, (LANES, 1), 0) * W
            expand = ((kpos >= jrow) & (kpos < jrow + W)).astype(BF16)
            sel_masks = [jnp.dot(sel_ref[0, rs].astype(BF16), expand, preferred_element_type=F32) > 0.5
                         for rs in range(Rs)]
        if mode != "fox":
            qf = qpos.astype(F32)
            kf = kpos.astype(F32)
        for r in range(R):
            rows = slice(r * tq, (r + 1) * tq)
            sr = s[rows]
            if mode == "fox":
                sr = sr + cq_scr[rows] - ck_ref[0, 0, r:r + 1, :]
                mask = base_mask
            else:
                slope = sl_ref[0, r:r + 1, 0:1]
                sr = sr + slope * kf - slope * qf
                mask = base_mask & sel_masks[r if Rs > 1 else 0] if mode == "sel" else base_mask
            sr = jnp.where(mask, sr, NEG)
            m_prev = m_scr[rows]
            m_new = jnp.maximum(m_prev, jnp.max(sr, axis=-1, keepdims=True))
            alpha = jnp.exp(m_prev - m_new)
            p = jnp.exp(sr - m_new)
            l_scr[rows] = alpha * l_scr[rows] + jnp.sum(p, axis=-1, keepdims=True)
            acc_scr[rows] = alpha * acc_scr[rows] + jnp.dot(p.astype(BF16), v, preferred_element_type=F32)
            m_scr[rows] = m_new

    @pl.when(ki == nk - 1)
    def _fin():
        vals = [acc_scr[r * tq:(r + 1) * tq] / l_scr[r * tq:(r + 1) * tq] for r in range(R)]
        for j, blk in enumerate(_assemble_heads(vals, rows_cfg, out_cfg, tq)):
            o_ref[0, :, j * LANES:(j + 1) * LANES] = blk


def flash_attention(mode, proj, *, batch, n, groups, q_blk, k_blk, v_blk, rows_cfg, out_cfg, out_width, out_blk,
                    tq, tk, W=None, slopes=None, sel=None, cum=None, name="flash"):
    R = len(rows_cfg)
    qw = (max(c[0] for c in rows_cfg) + 1) * LANES
    if mode == "win":
        assert tq == tk == NSA_WINDOW
        nk = 2
        kmap = lambda col: (lambda b, p, qi, ki: (b, jnp.maximum(qi - 1 + ki, 0), col(p)))
    else:
        nk = n // tk
        kmap = lambda col: (lambda b, p, qi, ki: (b, jnp.minimum(ki, (qi * tq + tq - 1) // tk), col(p)))
    in_specs, args = [], []
    Rs = 0
    if mode != "fox":
        in_specs.append(pl.BlockSpec((1, 8, LANES), lambda b, p, qi, ki: (p, 0, 0)))
        args.append(slopes)
    in_specs += [pl.BlockSpec((1, tq, qw), lambda b, p, qi, ki: (b, qi, q_blk(p))),
                 pl.BlockSpec((1, tk, LANES), kmap(k_blk)), pl.BlockSpec((1, tk, LANES), kmap(v_blk))]
    args += [proj, proj, proj]
    scratch = [pltpu.VMEM((R * tq, LANES), BF16), pltpu.VMEM((R * tq, 1), F32), pltpu.VMEM((R * tq, 1), F32),
               pltpu.VMEM((R * tq, LANES), F32)]
    if mode == "fox":
        in_specs += [pl.BlockSpec((1, 1, R, tq), lambda b, p, qi, ki: (b, p, 0, qi)),
                     pl.BlockSpec((1, 1, R, tk), lambda b, p, qi, ki: (b, p, 0, jnp.minimum(ki, (qi * tq + tq - 1) // tk)))]
        args += [cum, cum]
        scratch.append(pltpu.VMEM((R * tq, 1), F32))
    elif mode == "sel":
        Rs = sel.shape[1] // groups
        in_specs.append(pl.BlockSpec((1, Rs, tq, LANES), lambda b, p, qi, ki: (b, p, qi, 0)))
        args.append(sel)
    kern = functools.partial(_flash_kernel, mode=mode, R=R, Rs=Rs, tq=tq, tk=tk, W=W, rows_cfg=rows_cfg,
                             out_cfg=out_cfg, nk=nk)
    ow = (max(c[0] for c in out_cfg) + 1) * LANES
    return pl.pallas_call(
        kern, grid=(batch, groups, n // tq, nk), in_specs=in_specs,
        out_specs=pl.BlockSpec((1, tq, ow), lambda b, p, qi, ki: (b, qi, out_blk(p))),
        out_shape=jax.ShapeDtypeStruct((batch, n, out_width), F32), scratch_shapes=scratch,
        compiler_params=_cparams(("parallel", "parallel", "parallel", "arbitrary")), name=name)(*args)


def _kmean_kernel(k_ref, o_ref, *, nb):
    k = k_ref[0]
    km = jnp.sum(k.reshape(nb, MOBA_BLOCK, LANES), axis=1) * (1.0 / MOBA_BLOCK)
    o_ref[0, 0] = jnp.concatenate([km, jnp.zeros((LANES - nb, LANES), F32)], axis=0)


def moba_kmean(proj, batch, n):
    nb = n // MOBA_BLOCK
    pairs = MOBA_HEADS // 2
    kcol = EVEN_OFF[1] // LANES
    return pl.pallas_call(
        functools.partial(_kmean_kernel, nb=nb), grid=(batch, pairs),
        in_specs=[pl.BlockSpec((1, n, LANES), lambda b, p: (b, 0, kcol + p))],
        out_specs=pl.BlockSpec((1, 1, LANES, LANES), lambda b, p: (b, p, 0, 0)),
        out_shape=jax.ShapeDtypeStruct((batch, pairs, LANES, LANES), F32),
        compiler_params=_cparams(("parallel", "parallel")), name="moba_kmean")(proj)


def _top_lanes(x, count):
    lane = lax.broadcasted_iota(jnp.int32, x.shape, 1)
    chosen = jnp.zeros(x.shape, jnp.bool_)
    for _ in range(count):
        m = jnp.max(x, axis=-1, keepdims=True)
        idx = jnp.min(jnp.where(x == m, lane, LANES), axis=-1, keepdims=True)
        hit = lane == idx
        chosen = chosen | hit
        x = jnp.where(hit, -jnp.inf, x)
    return chosen


def _moba_gate_kernel(q_ref, km_ref, sel_ref, *, tq):
    q0 = pl.program_id(2) * tq
    lane = lax.broadcasted_iota(jnp.int32, (tq, LANES), 1)
    n_past = (q0 + lax.broadcasted_iota(jnp.int32, (tq, 1), 0)) // MOBA_BLOCK
    km = km_ref[0, 0].astype(BF16)
    q2 = q_ref[0]
    for r in range(2):
        qr = jnp.where((lane >= r * HEAD_DIM) & (lane < (r + 1) * HEAD_DIM), q2, 0.0).astype(BF16)
        gate = _nt_dot(qr, km)
        valid = lane < n_past
        top = _top_lanes(jnp.where(valid, gate, -jnp.inf), MOBA_TOPK)
        sel_ref[0, r] = ((top & valid) | (lane == n_past)).astype(F32)


def moba_gate(proj, kmean, batch, n, tq):
    pairs = MOBA_HEADS // 2
    qcol = EVEN_OFF[0] // LANES
    return pl.pallas_call(
        functools.partial(_moba_gate_kernel, tq=tq), grid=(batch, pairs, n // tq),
        in_specs=[pl.BlockSpec((1, tq, LANES), lambda b, p, qi: (b, qi, qcol + p)),
                  pl.BlockSpec((1, 1, LANES, LANES), lambda b, p, qi: (b, p, 0, 0))],
        out_specs=pl.BlockSpec((1, 2, tq, LANES), lambda b, p, qi: (b, p, qi, 0)),
        out_shape=jax.ShapeDtypeStruct((batch, MOBA_HEADS, n, LANES), F32),
        compiler_params=_cparams(("parallel", "parallel", "parallel")), name="moba_gate")(proj, kmean)


def _gelu(x):
    return 0.5 * x * (1.0 + lax.erf(x * (2.0 ** -0.5)))


def _compress_kernel(c_ref, pa_ref, pb_ref, w1a_ref, w1b_ref, w2_ref, o_ref):
    c = c_ref[0, 0]
    nch = c.shape[0]
    first = jnp.dot((c + pa_ref[0]).astype(BF16), w1a_ref[0], preferred_element_type=F32)
    second = jnp.dot((c + pb_ref[0]).astype(BF16), w1b_ref[0], preferred_element_type=F32)
    hidden = _gelu(first + pltpu.roll(second, nch - 1, axis=0))
    o_ref[0, 0] = jnp.dot(hidden.astype(BF16), w2_ref[0], preferred_element_type=F32)


def nsa_compress(chunks, pos_a, pos_b, w1a, w1b, w2):
    _, batch, nch, cw = chunks.shape
    hid = w1a.shape[2]
    return pl.pallas_call(
        _compress_kernel, grid=(2, batch),
        in_specs=[pl.BlockSpec((1, 1, nch, cw), lambda t, b: (t, b, 0, 0)),
                  pl.BlockSpec((1, 1, cw), lambda t, b: (t, 0, 0)), pl.BlockSpec((1, 1, cw), lambda t, b: (t, 0, 0)),
                  pl.BlockSpec((1, cw, hid), lambda t, b: (t, 0, 0)), pl.BlockSpec((1, cw, hid), lambda t, b: (t, 0, 0)),
                  pl.BlockSpec((1, hid, LANES), lambda t, b: (t, 0, 0))],
        out_specs=pl.BlockSpec((1, 1, nch, LANES), lambda t, b: (t, b, 0, 0)),
        out_shape=jax.ShapeDtypeStruct((2, batch, nch, LANES), F32),
        compiler_params=_cparams(("parallel", "parallel")), name="nsa_compress")(chunks, pos_a, pos_b, w1a, w1b, w2)


def _nsa_cmp_kernel(sl_ref, q_ref, kc_ref, vc_ref, oc_ref, sel_ref, q_scr, *, tq, nch, rows_cfg, out_cfg):
    q0 = pl.program_id(1) * tq
    _stack_queries(q_ref, q_scr, rows_cfg, tq)
    kc = kc_ref[0, 0].astype(BF16)
    vc = vc_ref[0, 0].astype(BF16)
    s = _nt_dot(q_scr[...], kc)
    qpos = q0 + lax.broadcasted_iota(jnp.int32, (tq, 1), 0)
    cmp_end = lax.broadcasted_iota(jnp.int32, (1, nch), 1) * NSA_CMP_STRIDE + (NSA_CMP_LEN - 1)
    mask = cmp_end <= qpos
    qf = qpos.astype(F32)
    ef = cmp_end.astype(F32)
    ci = lax.broadcasted_iota(jnp.int32, (nch, 1), 0) * NSA_CMP_STRIDE
    sj = lax.broadcasted_iota(jnp.int32, (1, LANES), 1) * NSA_SEL_BLOCK
    cover = ((ci < sj + NSA_SEL_BLOCK) & (ci + NSA_CMP_LEN > sj)).astype(BF16)
    imp = jnp.zeros((tq, LANES), F32)
    vals = []
    for r in range(4):
        slope = sl_ref[0, r:r + 1, 0:1]
        sr = jnp.where(mask, s[r * tq:(r + 1) * tq] + slope * ef - slope * qf, NEG)
        m = jnp.max(sr, axis=-1, keepdims=True)
        e = jnp.where(mask, jnp.exp(sr - m), 0.0)
        p = (e / jnp.maximum(jnp.sum(e, axis=-1, keepdims=True), 1e-30)).astype(BF16)
        vals.append(jnp.dot(p, vc, preferred_element_type=F32))
        imp = imp + jnp.dot(p, cover, preferred_element_type=F32)
    for j, blk in enumerate(_assemble_heads(vals, rows_cfg, out_cfg, tq)):
        oc_ref[0, :, j * LANES:(j + 1) * LANES] = blk
    lane = lax.broadcasted_iota(jnp.int32, (tq, LANES), 1)
    cur = qpos // NSA_SEL_BLOCK
    causal = lane <= cur
    forced = (lane == 0) | (lane == cur) | (lane == cur - 1)
    ranked = jnp.where(causal, jnp.where(forced, jnp.inf, imp), -jnp.inf)
    sel_ref[0, 0] = (_top_lanes(ranked, NSA_SEL_TOPK) & causal).astype(F32)


def nsa_cmp_select(proj, cmp, slopes, g, batch, n, tq):
    rows_cfg, out_cfg = _group_cfg(g)
    nch = cmp.shape[2]
    qcol = EVEN_OFF[3] // (2 * LANES) + g
    kern = functools.partial(_nsa_cmp_kernel, tq=tq, nch=nch, rows_cfg=rows_cfg, out_cfg=out_cfg)
    return pl.pallas_call(
        kern, grid=(batch, n // tq),
        in_specs=[pl.BlockSpec((1, 8, LANES), lambda b, qi: (0, 0, 0)),
                  pl.BlockSpec((1, tq, 2 * LANES), lambda b, qi: (b, qi, qcol)),
                  pl.BlockSpec((1, 1, nch, LANES), lambda b, qi: (0, b, 0, 0)),
                  pl.BlockSpec((1, 1, nch, LANES), lambda b, qi: (1, b, 0, 0))],
        out_specs=[pl.BlockSpec((1, tq, 2 * LANES), lambda b, qi: (b, qi, 0)),
                   pl.BlockSpec((1, 1, tq, LANES), lambda b, qi: (b, 0, qi, 0))],
        out_shape=[jax.ShapeDtypeStruct((batch, n, 2 * LANES), F32),
                   jax.ShapeDtypeStruct((batch, 1, n, LANES), F32)],
        scratch_shapes=[pltpu.VMEM((4 * tq, LANES), BF16)],
        compiler_params=_cparams(("parallel", "parallel")), name="nsa_cmp_select")(slopes, proj, cmp, cmp)


_PEER_CAND = [(i, j) for i in range(PEER_TOPK) for j in range(PEER_TOPK) if (i + 1) * (j + 1) <= PEER_TOPK]
_PEER_CAND_ROWS = _round_up(len(_PEER_CAND), 8)


def _top_rows(x, count):
    nrows = x.shape[0]
    rid = lax.broadcasted_iota(jnp.int32, x.shape, 0)
    rank = jnp.full(x.shape, float(count), F32)
    vals = []
    for t in range(count):
        m = jnp.max(x, axis=0, keepdims=True)
        idx = jnp.min(jnp.where(x == m, rid, nrows), axis=0, keepdims=True)
        hit = rid == idx
        rank = jnp.where(hit, float(t), rank)
        x = jnp.where(hit, -jnp.inf, x)
        vals.append(m)
    return vals, rank


def _peer_route_kernel(x_ref, wq_ref, sk_ref, r_ref, cand_scr, *, tm):
    q = jnp.dot(x_ref[...].astype(BF16), wq_ref[...], preferred_element_type=F32)
    half = PEER_DKEY // 2
    for h in range(PEER_HEADS):
        s1 = _nt_dot(sk_ref[0], q[:, (2 * h) * half:(2 * h + 1) * half].astype(BF16))
        s2 = _nt_dot(sk_ref[1], q[:, (2 * h + 1) * half:(2 * h + 2) * half].astype(BF16))
        v1, rank1 = _top_rows(s1, PEER_TOPK)
        v2, rank2 = _top_rows(s2, PEER_TOPK)
        cand_scr[...] = jnp.full(cand_scr.shape, -jnp.inf, F32)
        for c, (i, j) in enumerate(_PEER_CAND):
            cand_scr[c:c + 1, :] = v1[i] + v2[j]
        top, crank = _top_rows(cand_scr[...], PEER_TOPK)
        picked = crank < float(PEER_TOPK)
        zsum = jnp.zeros((1, tm), F32)
        for t in range(PEER_TOPK):
            zsum = zsum + jnp.exp(top[t] - top[0])
        cnt_of_key = jnp.zeros(s1.shape, F32)
        start = 0
        for i in range(PEER_TOPK):
            width = PEER_TOPK // (i + 1)
            cnt_i = jnp.sum(jnp.where(picked[start:start + width], 1.0, 0.0), axis=0, keepdims=True)
            cnt_of_key = jnp.where(rank1 == float(i), cnt_i, cnt_of_key)
            start += width
        r_ref[h, 0] = jnp.exp(s1 - v1[0]) / zsum
        r_ref[h, 1] = cnt_of_key
        r_ref[h, 2] = jnp.exp(s2 - v2[0])
        r_ref[h, 3] = rank2


def peer_route(x, wq, subkeys, tm):
    t = x.shape[0]
    return pl.pallas_call(
        functools.partial(_peer_route_kernel, tm=tm), grid=(t // tm,),
        in_specs=[pl.BlockSpec((tm, D_MODEL), lambda i: (i, 0)),
                  pl.BlockSpec((D_MODEL, PEER_HEADS * PEER_DKEY), lambda i: (0, 0)),
                  pl.BlockSpec((2, PEER_NKEYS, PEER_DKEY // 2), lambda i: (0, 0, 0))],
        out_specs=pl.BlockSpec((PEER_HEADS, 4, PEER_NKEYS, tm), lambda i: (0, 0, 0, i)),
        out_shape=jax.ShapeDtypeStruct((PEER_HEADS, 4, PEER_NKEYS, t), F32),
        scratch_shapes=[pltpu.VMEM((_PEER_CAND_ROWS, tm), F32)],
        compiler_params=_cparams(("parallel",)), name="peer_route")(x, wq, subkeys)


def _peer_expert_kernel(x_ref, r_ref, u_ref, vt_ref, g_ref, b_ref, y_ref, xb_scr, acc_scr, g_scr, *, tm, na, nchunks):
    j = pl.program_id(1)

    @pl.when(j == 0)
    def _init():
        xb_scr[...] = x_ref[...].astype(BF16)
        acc_scr[...] = jnp.zeros(acc_scr.shape, F32)

    act = _gelu(_nt_dot(u_ref[...], xb_scr[...]))
    for ai in range(na):
        a = j * na + ai
        w = jnp.zeros((PEER_NKEYS, tm), F32)
        for h in range(PEER_HEADS):
            e1 = r_ref[h, 0, pl.ds(a, 1), :]
            cnt = r_ref[h, 1, pl.ds(a, 1), :]
            w = w + jnp.where(r_ref[h, 3] < cnt, e1 * r_ref[h, 2], 0.0)
        rows = slice(ai * PEER_NKEYS, (ai + 1) * PEER_NKEYS)
        g_scr[rows, :] = (w * act[rows]).astype(BF16)
    acc_scr[...] += jnp.dot(vt_ref[...], g_scr[...], preferred_element_type=F32)

    @pl.when(j == nchunks - 1)
    def _fin():
        y_ref[...] = _layer_norm_rows(ALPHA * x_ref[...] + acc_scr[...].T, g_ref[...], b_ref[...])


def peer_experts(x, route, u, vt, g, b, tm, na):
    t = x.shape[0]
    ce = na * PEER_NKEYS
    nchunks = PEER_NKEYS // na
    kern = functools.partial(_peer_expert_kernel, tm=tm, na=na, nchunks=nchunks)
    return pl.pallas_call(
        kern, grid=(t // tm, nchunks),
        in_specs=[pl.BlockSpec((tm, D_MODEL), lambda i, j: (i, 0)),
                  pl.BlockSpec((PEER_HEADS, 4, PEER_NKEYS, tm), lambda i, j: (0, 0, 0, i)),
                  pl.BlockSpec((ce, D_MODEL), lambda i, j: (j, 0)),
                  pl.BlockSpec((D_MODEL, ce), lambda i, j: (0, j)),
                  pl.BlockSpec((1, D_MODEL), lambda i, j: (0, 0)), pl.BlockSpec((1, D_MODEL), lambda i, j: (0, 0))],
        out_specs=pl.BlockSpec((tm, D_MODEL), lambda i, j: (i, 0)),
        out_shape=jax.ShapeDtypeStruct((t, D_MODEL), F32),
        scratch_shapes=[pltpu.VMEM((tm, D_MODEL), BF16), pltpu.VMEM((D_MODEL, tm), F32), pltpu.VMEM((ce, tm), BF16)],
        compiler_params=_cparams(("parallel", "arbitrary")), name="peer_experts")(x, route, u, vt, g, b)


def _pad_cols(w, mult):
    return jnp.pad(w, ((0, 0), (0, _round_up(w.shape[1], mult) - w.shape[1])))


def _slope_table(head_slopes):
    arr = np.zeros((len(head_slopes), 8, LANES), np.float32)
    for p, sl in enumerate(head_slopes):
        for r, v in enumerate(sl):
            arr[p, r, :] = v
    return jnp.asarray(arr)


def _alibi(n):
    return [float(2.0 ** (-8.0 * (i + 1) / n)) for i in range(n)]


def _compress_weights(pos, w1, w2):
    half = NSA_CMP_LEN // 2
    eye = jnp.eye(NSA_KV_HEADS, dtype=F32)
    pos_g = jnp.broadcast_to(pos[:, :, None, :], (2, NSA_CMP_LEN, NSA_KV_HEADS, HEAD_DIM))
    pos_a = pos_g[:, :half].reshape(2, 1, -1)
    pos_b = pos_g[:, half:].reshape(2, 1, -1)
    w1r = w1.reshape(2, NSA_CMP_LEN, HEAD_DIM, NSA_CMP_HIDDEN)
    wide = jnp.einsum('tjdo,gh->tjgdho', w1r, eye)
    cw = half * NSA_KV_HEADS * HEAD_DIM
    w1a = wide[:, :half].reshape(2, cw, NSA_KV_HEADS * NSA_CMP_HIDDEN).astype(BF16)
    w1b = wide[:, half:].reshape(2, cw, NSA_KV_HEADS * NSA_CMP_HIDDEN).astype(BF16)
    w2bd = jnp.einsum('tod,gh->tgohd', w2, eye).reshape(2, NSA_KV_HEADS * NSA_CMP_HIDDEN,
                                                        NSA_KV_HEADS * HEAD_DIM).astype(BF16)
    return pos_a, pos_b, w1a, w1b, w2bd


def _moba_prompt(proj3, batch, n):
    pairs = MOBA_HEADS // 2
    rows_cfg, out_cfg = _pair_cfg()
    sl = _alibi(MOBA_HEADS)
    slopes = _slope_table([[sl[2 * p], sl[2 * p + 1]] for p in range(pairs)])
    kmean = moba_kmean(proj3, batch, n)
    sel = moba_gate(proj3, kmean, batch, n, tq=min(n, 512))
    q0, k0, v0 = (EVEN_OFF[i] // LANES for i in range(3))
    return flash_attention("sel", proj3, batch=batch, n=n, groups=pairs,
                           q_blk=lambda p: q0 + p, k_blk=lambda p: k0 + p, v_blk=lambda p: v0 + p,
                           rows_cfg=rows_cfg, out_cfg=out_cfg, out_width=MOBA_HEADS * HEAD_DIM, out_blk=lambda p: p,
                           tq=min(n, 512), tk=min(n, 512), W=MOBA_BLOCK, slopes=slopes, sel=sel, name="moba_attn")


def _nsa_prompt(proj3, batch, n, pos, w1, w2):
    sl = _alibi(NSA_HEADS)
    kv = proj3[:, :, EVEN_OFF[4]:EVEN_OFF[6]]
    chunks = jnp.stack([kv[:, :, :LANES], kv[:, :, LANES:]], 0).reshape(2, batch, n // NSA_CMP_STRIDE, -1)
    cmp = nsa_compress(chunks, *_compress_weights(pos, w1, w2))
    outs = [[], [], []]
    for g in range(NSA_KV_HEADS):
        rows_cfg, out_cfg = _group_cfg(g)
        slopes = _slope_table([sl[4 * g:4 * g + 4]])
        oc, sel = nsa_cmp_select(proj3, cmp, slopes, g, batch, n, tq=min(n, 256))
        common = dict(batch=batch, n=n, groups=1, q_blk=lambda p, g=g: EVEN_OFF[3] // (2 * LANES) + g,
                      rows_cfg=rows_cfg, out_cfg=out_cfg, out_width=2 * LANES, out_blk=lambda p: 0, slopes=slopes)
        osel = flash_attention("sel", proj3, k_blk=lambda p: EVEN_OFF[6] // LANES, v_blk=lambda p: EVEN_OFF[7] // LANES,
                               tq=min(n, 256), tk=min(n, 512), W=NSA_SEL_BLOCK, sel=sel, name="nsa_sel_attn", **common)
        owin = flash_attention("win", proj3, k_blk=lambda p: EVEN_OFF[8] // LANES, v_blk=lambda p: EVEN_OFF[9] // LANES,
                               tq=NSA_WINDOW, tk=NSA_WINDOW, name="nsa_win_attn", **common)
        for lst, o in zip(outs, (oc, osel, owin)):
            lst.append(o)
    return [jnp.concatenate(lst, -1) for lst in outs]


def _fox_prompt(proj3, cum, batch, n):
    pairs = FOX_HEADS // 2
    rows_cfg, out_cfg = _pair_cfg()
    q0, k0, v0 = (ODD_OFF[i] // LANES for i in range(3))
    cum_rows = cum.swapaxes(1, 2).reshape(batch, pairs, 2, n)
    return flash_attention("fox", proj3, batch=batch, n=n, groups=pairs,
                           q_blk=lambda p: q0 + p, k_blk=lambda p: k0 + p, v_blk=lambda p: v0 + p,
                           rows_cfg=rows_cfg, out_cfg=out_cfg, out_width=FOX_HEADS * HEAD_DIM, out_blk=lambda p: p,
                           tq=min(n, 512), tk=min(n, 512), cum=cum_rows, name="fox_attn")


def _peer_layer(h, lw, tm):
    route = peer_route(h, lw['peer_wq'], lw['peer_sk'], tm)
    return peer_experts(h, route, lw['peer_u'], lw['peer_vt'], lw['ln_ffn_g'], lw['ln_ffn_b'], tm, na=8)


def _prompt_trunk(x, lws):
    batch, n, d = x.shape
    t = batch * n
    h = x.reshape(t, d)
    tm = 512
    moba_rows, nsa_rows, win_rows, fox_rows, logf_rows = [], [], [], [], []
    for layer, lw in enumerate(lws):
        proj = matmul(h, lw['w_in'], tm, 256)
        proj3 = proj.reshape(batch, n, -1)
        if layer % 2 == 0:
            oa = _moba_prompt(proj3, batch, n)
            oc, osel, owin = _nsa_prompt(proj3, batch, n, lw['cmp_pos'], lw['cmp_w1'], lw['cmp_w2'])
            hw = MOBA_HEADS * HEAD_DIM
            h = out_proj_even(oa.reshape(t, hw), oc.reshape(t, hw), osel.reshape(t, hw), owin.reshape(t, hw), proj, h,
                              lw['w_out'], lw['ln_mix_g'], lw['ln_mix_b'], tm=256)
            moba_rows.append(proj3[:, :, EVEN_OFF[1]:EVEN_OFF[3]].reshape(batch, n, 2, MOBA_HEADS, HEAD_DIM))
            nsa_rows.append(proj3[:, :, EVEN_OFF[4]:EVEN_OFF[8]].reshape(batch, n, 4, NSA_KV_HEADS, HEAD_DIM))
            keep = min(NSA_WINDOW, n)
            win_rows.append(proj3[:, n - keep:, EVEN_OFF[8]:EVEN_OFF[10]].reshape(batch, keep, 2, NSA_KV_HEADS, HEAD_DIM))
        else:
            logf = jax.nn.log_sigmoid(proj3[:, :, ODD_OFF[3]:ODD_OFF[4]] + lw['b_forget'])
            cum = jnp.cumsum(logf, axis=1)
            o = _fox_prompt(proj3, cum, batch, n)
            h = out_proj_odd(o.reshape(t, d), h, lw['w_out'], lw['ln_mix_g'], lw['ln_mix_b'], tm=256)
            fox_rows.append(proj3[:, :, ODD_OFF[1]:ODD_OFF[3]].reshape(batch, n, 2, FOX_HEADS, HEAD_DIM))
            logf_rows.append(logf)
        h = _peer_layer(h, lw, tm)
    return (h.reshape(batch, n, d), jnp.stack(moba_rows, 1), jnp.stack(nsa_rows, 1), jnp.stack(win_rows, 1),
            jnp.stack(fox_rows, 1), jnp.stack(logf_rows, 1))


def _jx_masked_softmax(s, mask):
    s = jnp.where(mask, s, -jnp.inf)
    m = jnp.max(s, -1, keepdims=True)
    m = jnp.where(jnp.isfinite(m), m, 0.0)
    e = jnp.exp(s - m)
    return e / jnp.maximum(e.sum(-1, keepdims=True), 1e-30)


def _jx_pad_rows(a, n):
    return jnp.pad(a, [(0, 0), (0, n - a.shape[1])] + [(0, 0)] * (a.ndim - 2))


def _jx_gather_pages(pool, page_table, layer):
    g = pool[page_table, layer]
    return g.reshape((g.shape[0], -1) + g.shape[3:])


def _jx_moba_step(q, k_new, v_new, k_past, v_past):
    b, nq, h, dh = q.shape
    past = k_past.shape[1]
    nblk = max(-(-(past + nq) // MOBA_BLOCK), MOBA_TOPK)
    k = _jx_pad_rows(jnp.concatenate([k_past, k_new], 1), nblk * MOBA_BLOCK)
    v = _jx_pad_rows(jnp.concatenate([v_past, v_new], 1), nblk * MOBA_BLOCK)
    k_blk = k.reshape(b, nblk, MOBA_BLOCK, h, dh).transpose(0, 3, 1, 2, 4)
    v_blk = v.reshape(b, nblk, MOBA_BLOCK, h, dh).transpose(0, 3, 1, 2, 4)
    k_mean = k_blk.mean(3)
    slopes = jnp.asarray(_alibi(h), F32)
    bi = jnp.arange(b)[:, None, None, None]
    hi = jnp.arange(h)[None, None, :, None]
    blk_ids = jnp.arange(nblk)
    offs = jnp.arange(MOBA_BLOCK)
    qpos = past + jnp.arange(nq)
    n_past = qpos // MOBA_BLOCK
    gate = jnp.einsum('bqhd,bhnd->bqhn', q, k_mean)
    gate = jnp.where((blk_ids[None, :] < n_past[:, None])[None, :, None, :], gate, -jnp.inf)
    _, top = lax.top_k(gate, MOBA_TOPK)
    own = jnp.broadcast_to(n_past[None, :, None, None], (b, nq, h, 1)).astype(top.dtype)
    idx = jnp.concatenate([top, own], -1)
    slot_ok = jnp.concatenate([jnp.arange(MOBA_TOPK)[None, :] < n_past[:, None], jnp.ones((nq, 1), bool)], -1)
    kg = k_blk[bi, hi, idx]
    vg = v_blk[bi, hi, idx]
    dist = qpos[None, :, None, None, None] - (idx[..., None] * MOBA_BLOCK + offs)
    s = jnp.einsum('bqhd,bqhjkd->bqhjk', q, kg) * SCALE - slopes[None, None, :, None, None] * dist
    mask = (dist >= 0) & slot_ok[None, :, None, :, None]
    p = _jx_masked_softmax(s.reshape(b, nq, h, -1), mask.reshape(b, nq, h, -1))
    return jnp.einsum('bqhn,bqhnd->bqhd', p, vg.reshape(b, nq, h, -1, dh))


def _jx_nsa_compress(k, pos, w1, w2):
    b, n, g, dh = k.shape
    r = NSA_CMP_LEN // NSA_CMP_STRIDE
    kd = k.reshape(b, n // NSA_CMP_STRIDE, NSA_CMP_STRIDE, g, dh)
    nc = n // NSA_CMP_STRIDE - r + 1
    blocks = jnp.concatenate([kd[:, i:i + nc] for i in range(r)], 2) + pos[:, None, :]
    flat = blocks.transpose(0, 1, 3, 2, 4).reshape(b, nc, g, NSA_CMP_LEN * dh)
    return jax.nn.gelu(flat @ w1, approximate=False) @ w2


def _jx_nsa_step(q, gates, kc_new, vc_new, ks_new, vs_new, kw_new, vw_new,
                 kc_past, vc_past, ks_past, vs_past, kw_buf, vw_buf, cmp_pos, cmp_w1, cmp_w2):
    b, nq, h, dh = q.shape
    g = kc_new.shape[2]
    r = h // g
    past = kc_past.shape[1]
    n_pad = max(-(-(past + nq) // NSA_SEL_BLOCK), NSA_SEL_TOPK) * NSA_SEL_BLOCK
    kc = _jx_pad_rows(jnp.concatenate([kc_past, kc_new], 1), n_pad)
    vc = _jx_pad_rows(jnp.concatenate([vc_past, vc_new], 1), n_pad)
    ks = _jx_pad_rows(jnp.concatenate([ks_past, ks_new], 1), n_pad)
    vs = _jx_pad_rows(jnp.concatenate([vs_past, vs_new], 1), n_pad)
    k_cmp = _jx_nsa_compress(kc, cmp_pos[0], cmp_w1[0], cmp_w2[0])
    v_cmp = _jx_nsa_compress(vc, cmp_pos[1], cmp_w1[1], cmp_w2[1])
    cmp_start = jnp.arange(k_cmp.shape[1]) * NSA_CMP_STRIDE
    cmp_end = cmp_start + NSA_CMP_LEN - 1
    n_sel = n_pad // NSA_SEL_BLOCK
    sel_start = jnp.arange(n_sel) * NSA_SEL_BLOCK
    cover = ((cmp_start[:, None] < sel_start[None, :] + NSA_SEL_BLOCK)
             & (cmp_start[:, None] + NSA_CMP_LEN > sel_start[None, :])).astype(F32)
    ks_blk = ks.reshape(b, n_sel, NSA_SEL_BLOCK, g, dh).transpose(0, 3, 1, 2, 4)
    vs_blk = vs.reshape(b, n_sel, NSA_SEL_BLOCK, g, dh).transpose(0, 3, 1, 2, 4)
    win_start = past - kw_buf.shape[1]
    zpad = jnp.zeros((b, NSA_WINDOW, g, dh), kw_new.dtype)
    kw_src = jnp.concatenate([zpad, kw_buf, kw_new], 1)
    vw_src = jnp.concatenate([zpad, vw_buf, vw_new], 1)
    slopes = jnp.asarray(_alibi(h), F32).reshape(g, r)
    bi = jnp.arange(b)[:, None, None, None]
    gi = jnp.arange(g)[None, None, :, None]
    sel_ids = jnp.arange(n_sel)
    sel_offs = jnp.arange(NSA_SEL_BLOCK)
    qpos = past + jnp.arange(nq)
    qg = q.reshape(b, nq, g, r, dh)
    d_c = qpos[:, None] - cmp_end[None, :]
    s_c = jnp.einsum('bqgrd,bngd->bqgrn', qg, k_cmp) * SCALE - slopes[:, :, None] * d_c[:, None, None, :]
    p_c = _jx_masked_softmax(s_c, (d_c >= 0)[:, None, None, :])
    o_c = jnp.einsum('bqgrn,bngd->bqgrd', p_c, v_cmp)
    imp = jnp.einsum('bqgrn,nm->bqgm', p_c, cover)
    cur = qpos // NSA_SEL_BLOCK
    causal = sel_ids[None, :] <= cur[:, None]
    forced = ((sel_ids[None, :] == 0) | (sel_ids[None, :] == cur[:, None]) | (sel_ids[None, :] == cur[:, None] - 1))
    imp = jnp.where(causal[None, :, None, :], jnp.where(forced[None, :, None, :], jnp.inf, imp), -jnp.inf)
    _, sidx = lax.top_k(imp, NSA_SEL_TOPK)
    slot_ok = jnp.arange(NSA_SEL_TOPK)[None, :] < (cur + 1)[:, None]
    kg = ks_blk[bi, gi, sidx]
    vg = vs_blk[bi, gi, sidx]
    d_s = qpos[None, :, None, None, None] - (sidx[..., None] * NSA_SEL_BLOCK + sel_offs)
    s_s = (jnp.einsum('bqgrd,bqgnkd->bqgrnk', qg, kg) * SCALE
           - slopes[None, None, :, :, None, None] * d_s[:, :, :, None])
    m_s = jnp.broadcast_to(((d_s >= 0) & slot_ok[None, :, None, :, None])[:, :, :, None], s_s.shape)
    p_s = _jx_masked_softmax(s_s.reshape(b, nq, g, r, -1), m_s.reshape(b, nq, g, r, -1))
    o_s = jnp.einsum('bqgrn,bqgnd->bqgrd', p_s, vg.reshape(b, nq, g, -1, dh))
    span = NSA_WINDOW + nq
    kw = lax.dynamic_slice_in_dim(kw_src, qpos[0] - win_start, span, axis=1)
    vw = lax.dynamic_slice_in_dim(vw_src, qpos[0] - win_start, span, axis=1)
    kw_pos = qpos[0] - NSA_WINDOW + jnp.arange(span)
    d_w = qpos[:, None] - kw_pos[None, :]
    m_w = (d_w >= 0) & (d_w <= NSA_WINDOW) & (kw_pos >= win_start)[None, :]
    s_w = jnp.einsum('bqgrd,bkgd->bqgrk', qg, kw) * SCALE - slopes[:, :, None] * d_w[:, None, None, :]
    p_w = _jx_masked_softmax(s_w, m_w[:, None, None, :])
    o_w = jnp.einsum('bqgrk,bkgd->bqgrd', p_w, vw)
    gg = gates.reshape(b, nq, g, r, 3)
    o = gg[..., 0:1] * o_c + gg[..., 1:2] * o_s + gg[..., 2:3] * o_w
    kw_all = jnp.concatenate([kw_buf, kw_new], 1)
    vw_all = jnp.concatenate([vw_buf, vw_new], 1)
    keep = min(NSA_WINDOW, kw_all.shape[1])
    new_win = jnp.stack([kw_all[:, kw_all.shape[1] - keep:], vw_all[:, vw_all.shape[1] - keep:]], 2)
    return o.reshape(b, nq, h, dh), new_win


def _jx_fox_step(q, k_new, v_new, logf_new, k_past, v_past, logf_past):
    past = k_past.shape[1]
    nq = q.shape[1]
    k = jnp.concatenate([k_past, k_new], 1)
    v = jnp.concatenate([v_past, v_new], 1)
    cum = jnp.cumsum(jnp.concatenate([logf_past, logf_new], 1), axis=1)
    cum_k = cum.swapaxes(1, 2)[:, None]
    kpos = jnp.arange(past + nq)
    qpos = past + jnp.arange(nq)
    s = jnp.einsum('bqhd,bkhd->bqhk', q, k) * SCALE + cum[:, past:][..., None] - cum_k
    p = _jx_masked_softmax(s, (kpos[None, :] <= qpos[:, None])[None, :, None, :])
    return jnp.einsum('bqhk,bkhd->bqhd', p, v)


def _sample_trunk(x, lws, caches, page_table):
    cache_moba, cache_nsa, state_win, cache_fox, cache_logf = caches
    batch, n, d = x.shape
    t = batch * n
    h = x.reshape(t, d)
    hd = lambda a: a.reshape(batch, n, -1, HEAD_DIM)
    moba_rows, nsa_rows, win_rows, fox_rows, logf_rows = [], [], [], [], []
    for layer, lw in enumerate(lws):
        i = layer // 2
        proj = matmul(h, lw['w_in'], t, 256)
        p3 = proj.reshape(batch, n, -1)
        if layer % 2 == 0:
            qa, ka, va, qb, kc, vc, ks, vs, kw, vw = (hd(p3[:, :, EVEN_OFF[j]:EVEN_OFF[j + 1]]) for j in range(10))
            gl = p3[:, :, EVEN_OFF[10]:EVEN_OFF[11]]
            mp = _jx_gather_pages(cache_moba, page_table, i)
            npst = _jx_gather_pages(cache_nsa, page_table, i)
            wp = state_win[:, i]
            o_a = _jx_moba_step(qa, ka, va, mp[:, :, 0], mp[:, :, 1])
            o_b, win = _jx_nsa_step(qb, jax.nn.sigmoid(gl.reshape(batch, n, NSA_HEADS, 3)), kc, vc, ks, vs, kw, vw,
                                    npst[:, :, 0], npst[:, :, 1], npst[:, :, 2], npst[:, :, 3],
                                    wp[:, :, 0], wp[:, :, 1], lw['cmp_pos'], lw['cmp_w1'], lw['cmp_w2'])
            mixed = jnp.concatenate([o_a.reshape(t, -1), o_b.reshape(t, -1)], -1)
            h = out_proj_odd(mixed, h, lw['w_out'], lw['ln_mix_g'], lw['ln_mix_b'], tm=t)
            moba_rows.append(jnp.stack([ka, va], 2))
            nsa_rows.append(jnp.stack([kc, vc, ks, vs], 2))
            win_rows.append(win)
        else:
            qc, kc, vc = (hd(p3[:, :, ODD_OFF[j]:ODD_OFF[j + 1]]) for j in range(3))
            logf = jax.nn.log_sigmoid(p3[:, :, ODD_OFF[3]:ODD_OFF[4]] + lw['b_forget'])
            fp = _jx_gather_pages(cache_fox, page_table, i)
            o_c = _jx_fox_step(qc, kc, vc, logf, fp[:, :, 0], fp[:, :, 1], _jx_gather_pages(cache_logf, page_table, i))
            h = out_proj_odd(o_c.reshape(t, -1), h, lw['w_out'], lw['ln_mix_g'], lw['ln_mix_b'], tm=t)
            fox_rows.append(jnp.stack([kc, vc], 2))
            logf_rows.append(logf)
        h = _peer_layer(h, lw, t)
    return (h.reshape(batch, n, d), jnp.stack(moba_rows, 1), jnp.stack(nsa_rows, 1), jnp.stack(win_rows, 1),
            jnp.stack(fox_rows, 1), jnp.stack(logf_rows, 1))


def _layer_weights(layer, w_in_ab, w_out_ab, nsa_cmp_pos, nsa_cmp_w1, nsa_cmp_w2, w_in_c, b_forget, w_out_c,
                   ln_mix_g, ln_mix_b, ln_ffn_g, ln_ffn_b, peer_wq, peer_subkeys, peer_u, peer_v):
    i = layer // 2
    lw = dict(ln_mix_g=ln_mix_g[layer][None], ln_mix_b=ln_mix_b[layer][None],
              ln_ffn_g=ln_ffn_g[layer][None], ln_ffn_b=ln_ffn_b[layer][None],
              peer_wq=peer_wq[layer].astype(BF16), peer_sk=peer_subkeys[layer].astype(BF16),
              peer_u=peer_u[layer].astype(BF16), peer_vt=peer_v[layer].astype(BF16).T)
    if layer % 2 == 0:
        lw.update(w_in=_pad_cols(w_in_ab[i], 256).astype(BF16), w_out=w_out_ab[i].astype(BF16),
                  cmp_pos=nsa_cmp_pos[i], cmp_w1=nsa_cmp_w1[i], cmp_w2=nsa_cmp_w2[i])
    else:
        lw.update(w_in=_pad_cols(w_in_c[i], 256).astype(BF16), w_out=w_out_c[i].astype(BF16), b_forget=b_forget[i])
    return lw


def kernel(x_prompt, x_sample, cache_moba_kv, cache_nsa_kv, state_nsa_win, cache_fox_kv, cache_fox_logf, page_table,
           w_in_ab, w_out_ab, nsa_cmp_pos, nsa_cmp_w1, nsa_cmp_w2, w_in_c, b_forget, w_out_c,
           ln_mix_g, ln_mix_b, ln_ffn_g, ln_ffn_b, peer_wq, peer_subkeys, peer_u, peer_v):
    lws = [_layer_weights(layer, w_in_ab, w_out_ab, nsa_cmp_pos, nsa_cmp_w1, nsa_cmp_w2, w_in_c, b_forget, w_out_c,
                          ln_mix_g, ln_mix_b, ln_ffn_g, ln_ffn_b, peer_wq, peer_subkeys, peer_u, peer_v)
           for layer in range(DEPTH)]
    y_p, moba_p, nsa_p, win_p, fox_p, logf_p = _prompt_trunk(x_prompt, lws)
    y_s, moba_s, nsa_s, win_s, fox_s, logf_s = _sample_trunk(
        x_sample, lws, (cache_moba_kv, cache_nsa_kv, state_nsa_win, cache_fox_kv, cache_fox_logf), page_table)
    return (y_p, y_s, moba_p, moba_s, nsa_p, nsa_s, win_p, win_s, fox_p, fox_s, logf_p, logf_s)
```

```python
import functools
import math

import numpy as np
import jax
import jax.numpy as jnp
from jax import lax
from jax.experimental import pallas as pl
from jax.experimental.pallas import tpu as pltpu

F32 = jnp.float32
BF16 = jnp.bfloat16

D_MODEL = 1024
DEPTH = 4
PAGE_SIZE = 128
HEAD_DIM = 64
LANES = 128
MOBA_HEADS = 8
MOBA_BLOCK = 256
MOBA_TOPK = 3
MOBA_QBLK = 32
NSA_HEADS = 8
NSA_KV_HEADS = 2
NSA_CMP_LEN = 32
NSA_CMP_STRIDE = 16
NSA_CMP_HIDDEN = 128
NSA_SEL_BLOCK = 64
NSA_SEL_TOPK = 16
NSA_WINDOW = 512
FOX_HEADS = 16
QBLK = 128
PEER_HEADS = 8
PEER_NKEYS = 128
PEER_EXPERTS = PEER_NKEYS * PEER_NKEYS
PEER_TOPK = 16
PEER_DKEY = 256
PEER_TBLK = 128
N_EVEN = (DEPTH + 1) // 2
N_ODD = DEPTH // 2
ALPHA = (2 * DEPTH) ** 0.25
LN_EPS = 1e-5
EVEN_COLS = (MOBA_HEADS * HEAD_DIM,) * 3 + (NSA_HEADS * HEAD_DIM,) + (NSA_KV_HEADS * HEAD_DIM,) * 6 + (NSA_HEADS * 3,)
ODD_COLS = (FOX_HEADS * HEAD_DIM,) * 3 + (FOX_HEADS,)
EVEN_OFF = tuple(int(v) for v in np.cumsum((0,) + EVEN_COLS))
ODD_OFF = tuple(int(v) for v in np.cumsum((0,) + ODD_COLS))
SCALE = HEAD_DIM ** -0.5
NEG = -1e30
VMEM_LIMIT = 56 * 1024 * 1024
FLASH_ROW_CHUNK = 512


def _cparams(sem):
    return pltpu.CompilerParams(dimension_semantics=sem, vmem_limit_bytes=VMEM_LIMIT)


def _round_up(n, m):
    return -(-n // m) * m


def _nt_dot(a, b):
    return lax.dot_general(a, b, (((1,), (1,)), ((), ())), preferred_element_type=F32)


def _mm_kernel(x_ref, w_ref, o_ref):
    o_ref[...] = jnp.dot(x_ref[...].astype(BF16), w_ref[...], preferred_element_type=F32)


def matmul(x, w, tm, tn):
    m, k = x.shape
    n = w.shape[1]
    return pl.pallas_call(
        _mm_kernel, grid=(m // tm, n // tn),
        in_specs=[pl.BlockSpec((tm, k), lambda i, j: (i, 0)), pl.BlockSpec((k, tn), lambda i, j: (0, j))],
        out_specs=pl.BlockSpec((tm, tn), lambda i, j: (i, j)),
        out_shape=jax.ShapeDtypeStruct((m, n), F32),
        compiler_params=_cparams(("parallel", "parallel")), name="proj_matmul")(x, w)


def _layer_norm_rows(z, g, b):
    mu = jnp.mean(z, axis=-1, keepdims=True)
    zc = z - mu
    var = jnp.mean(zc * zc, axis=-1, keepdims=True)
    return zc * lax.rsqrt(var + LN_EPS) * g + b


def _out_even_kernel(oa_ref, oc_ref, os_ref, ow_ref, gl_ref, x_ref, w_ref, g_ref, b_ref, y_ref):
    sig = jax.nn.sigmoid(gl_ref[...])
    width = NSA_HEADS * HEAD_DIM
    head_of_lane = lax.broadcasted_iota(jnp.int32, (1, width), 1) // HEAD_DIM

    def expand(j):
        acc = jnp.zeros((sig.shape[0], width), F32)
        for h in range(NSA_HEADS):
            acc = jnp.where(head_of_lane == h, sig[:, 3 * h + j:3 * h + j + 1], acc)
        return acc

    ob = expand(0) * oc_ref[...] + expand(1) * os_ref[...] + expand(2) * ow_ref[...]
    half = MOBA_HEADS * HEAD_DIM
    mixed = (jnp.dot(oa_ref[...].astype(BF16), w_ref[0:half, :], preferred_element_type=F32)
             + jnp.dot(ob.astype(BF16), w_ref[half:, :], preferred_element_type=F32))
    y_ref[...] = _layer_norm_rows(ALPHA * x_ref[...] + mixed, g_ref[...], b_ref[...])


def out_proj_even(oa, oc, os_, ow, proj, x, w, g, b, tm):
    m = x.shape[0]
    hw = MOBA_HEADS * HEAD_DIM
    row = lambda i: (i, 0)
    fixed = lambda i: (0, 0)
    return pl.pallas_call(
        _out_even_kernel, grid=(m // tm,),
        in_specs=[pl.BlockSpec((tm, hw), row)] * 4
        + [pl.BlockSpec((tm, LANES), lambda i: (i, EVEN_OFF[10] // LANES)),
           pl.BlockSpec((tm, D_MODEL), row), pl.BlockSpec((2 * hw, D_MODEL), fixed),
           pl.BlockSpec((1, D_MODEL), fixed), pl.BlockSpec((1, D_MODEL), fixed)],
        out_specs=pl.BlockSpec((tm, D_MODEL), row),
        out_shape=jax.ShapeDtypeStruct((m, D_MODEL), F32),
        compiler_params=_cparams(("parallel",)), name="out_proj_even")(oa, oc, os_, ow, proj, x, w, g, b)


def _out_odd_kernel(o_ref, x_ref, w_ref, g_ref, b_ref, y_ref):
    mixed = jnp.dot(o_ref[...].astype(BF16), w_ref[...], preferred_element_type=F32)
    y_ref[...] = _layer_norm_rows(ALPHA * x_ref[...] + mixed, g_ref[...], b_ref[...])


def out_proj_odd(o, x, w, g, b, tm):
    m = x.shape[0]
    row = lambda i: (i, 0)
    fixed = lambda i: (0, 0)
    return pl.pallas_call(
        _out_odd_kernel, grid=(m // tm,),
        in_specs=[pl.BlockSpec((tm, D_MODEL), row), pl.BlockSpec((tm, D_MODEL), row),
                  pl.BlockSpec((D_MODEL, D_MODEL), fixed),
                  pl.BlockSpec((1, D_MODEL), fixed), pl.BlockSpec((1, D_MODEL), fixed)],
        out_specs=pl.BlockSpec((tm, D_MODEL), row),
        out_shape=jax.ShapeDtypeStruct((m, D_MODEL), F32),
        compiler_params=_cparams(("parallel",)), name="out_proj_odd")(o, x, w, g, b)


def _stack_queries(q_ref, q_scr, rows_cfg, tq):
    lane = lax.broadcasted_iota(jnp.int32, (tq, LANES), 1)
    for r, (qb, qh, kh) in enumerate(rows_cfg):
        x = q_ref[0, :, qb * LANES:(qb + 1) * LANES]
        if qh != kh:
            x = pltpu.roll(x, HEAD_DIM, axis=1)
        x = jnp.where((lane >= kh * HEAD_DIM) & (lane < (kh + 1) * HEAD_DIM), x * SCALE, 0.0)
        q_scr[r * tq:(r + 1) * tq, :] = x.astype(BF16)


def _assemble_heads(vals, rows_cfg, out_cfg, tq):
    lane = lax.broadcasted_iota(jnp.int32, (tq, LANES), 1)
    nblk = max(ob for ob, _ in out_cfg) + 1
    blocks = []
    for j in range(nblk):
        parts = {}
        for r, (ob, oh) in enumerate(out_cfg):
            if ob != j:
                continue
            x = vals[r]
            if rows_cfg[r][2] != oh:
                x = pltpu.roll(x, HEAD_DIM, axis=1)
            parts[oh] = x
        blocks.append(jnp.where(lane < HEAD_DIM, parts[0], parts[1]))
    return blocks


def _pair_cfg():
    return [(0, 0, 0), (0, 1, 1)], [(0, 0), (0, 1)]


def _group_cfg(g):
    rows = [(r // 2, r % 2, g) for r in range(4)]
    outs = [(r // 2, r % 2) for r in range(4)]
    return rows, outs


def _flash_kernel(*refs, mode, R, Rs, tq, tk, W, rows_cfg, out_cfg, nk):
    if mode == "fox":
        q_ref, k_ref, v_ref, cq_ref, ck_ref, o_ref, q_scr, m_scr, acc_scr, cq_scr = refs
    elif mode == "sel":
        sl_ref, q_ref, k_ref, v_ref, sel_ref, o_ref, q_scr, m_scr, acc_scr = refs
    else:
        sl_ref, q_ref, k_ref, v_ref, o_ref, q_scr, m_scr, acc_scr = refs
    qi = pl.program_id(2)
    ki = pl.program_id(3)
    q0 = qi * tq

    @pl.when(ki == 0)
    def _init():
        _stack_queries(q_ref, q_scr, rows_cfg, tq)
        m_scr[...] = jnp.full(m_scr.shape, NEG, F32)
        acc_scr[...] = jnp.zeros(acc_scr.shape, F32)
        if mode == "fox":
            eye = (lax.broadcasted_iota(jnp.int32, (tq, tq), 0) == lax.broadcasted_iota(jnp.int32, (tq, tq), 1))
            for r in range(R):
                row = cq_ref[0, 0, r:r + 1, :]
                cq_scr[r * tq:(r + 1) * tq, :] = jnp.sum(jnp.where(eye, row, 0.0), axis=1, keepdims=True)

    if mode == "win":
        kt = qi - 1 + ki
        active = kt >= 0
        k0 = kt * tk
    else:
        active = ki * tk <= q0 + tq - 1
        k0 = ki * tk

    ch = min(tq, FLASH_ROW_CHUNK)

    @pl.when(active)
    def _step():
        k = k_ref[0].astype(BF16)
        vraw = v_ref[0]
        lane_v = lax.broadcasted_iota(jnp.int32, (tk, LANES), 1)
        v_half = [jnp.where((lane_v >= kh * HEAD_DIM) & (lane_v < (kh + 1) * HEAD_DIM), vraw, 1.0).astype(BF16)
                  for kh in range(2)]
        kpos = k0 + lax.broadcasted_iota(jnp.int32, (1, tk), 1)
        if mode == "sel":
            jrow = lax.broadcasted_iota(jnp.int32, (LANES, 1), 0) * W
            expand = ((kpos >= jrow) & (kpos < jrow + W)).astype(BF16)
        if mode != "fox":
            kf = kpos.astype(F32)
        for r in range(R):
            v = v_half[rows_cfg[r][2]]
            if mode == "fox":
                ck = ck_ref[0, 0, r:r + 1, :]
            else:
                slope = sl_ref[0, r:r + 1, 0:1]
                kbias = slope * kf

            def chunk(c, carry, r=r, v=v):
                off = c * ch
                rows = pl.ds(r * tq + off, ch)
                sr = _nt_dot(q_scr[rows, :], k)
                qpos = q0 + off + lax.broadcasted_iota(jnp.int32, (ch, 1), 0)
                if mode == "fox":
                    sr = sr + cq_scr[rows, :] - ck
                else:
                    sr = sr + kbias - slope * qpos.astype(F32)
                if mode == "win":
                    d = qpos - kpos
                    mask = (d >= 0) & (d <= NSA_WINDOW)
                else:
                    mask = kpos <= qpos
                if mode == "sel":
                    picked = jnp.dot(sel_ref[0, r if Rs > 1 else 0, pl.ds(off, ch), :].astype(BF16), expand,
                                     preferred_element_type=F32)
                    mask = mask & (picked > 0.5)
                sr = jnp.where(mask, sr, NEG)
                m_prev = m_scr[rows, :]
                m_new = jnp.maximum(m_prev, jnp.max(sr, axis=-1, keepdims=True))
                p = jnp.exp(sr - m_new)
                acc_scr[rows, :] = (jnp.exp(m_prev - m_new) * acc_scr[rows, :]
                                    + jnp.dot(p.astype(BF16), v, preferred_element_type=F32))
                m_scr[rows, :] = m_new
                return carry

            for c in range(tq // ch):
                chunk(c, 0)

    @pl.when(ki == nk - 1)
    def _fin():
        vals = []
        for r in range(R):
            acc = acc_scr[r * tq:(r + 1) * tq]
            vals.append(acc / pltpu.roll(acc, HEAD_DIM, axis=1))
        for j, blk in enumerate(_assemble_heads(vals, rows_cfg, out_cfg, tq)):
            o_ref[0, :, j * LANES:(j + 1) * LANES] = blk


def flash_attention(mode, proj, *, batch, n, groups, q_blk, k_blk, v_blk, rows_cfg, out_cfg, out_width, out_blk,
                    tq, tk, W=None, slopes=None, sel=None, cum=None, name="flash"):
    R = len(rows_cfg)
    qw = (max(c[0] for c in rows_cfg) + 1) * LANES
    if mode == "win":
        assert tq == tk == NSA_WINDOW
        nk = 2
        kmap = lambda col: (lambda b, p, qi, ki: (b, jnp.maximum(qi - 1 + ki, 0), col(p)))
    else:
        nk = n // tk
        kmap = lambda col: (lambda b, p, qi, ki: (b, jnp.minimum(ki, (qi * tq + tq - 1) // tk), col(p)))
    in_specs, args = [], []
    Rs = 0
    if mode != "fox":
        in_specs.append(pl.BlockSpec((1, 8, LANES), lambda b, p, qi, ki: (p, 0, 0)))
        args.append(slopes)
    in_specs += [pl.BlockSpec((1, tq, qw), lambda b, p, qi, ki: (b, qi, q_blk(p))),
                 pl.BlockSpec((1, tk, LANES), kmap(k_blk)), pl.BlockSpec((1, tk, LANES), kmap(v_blk))]
    args += [proj, proj, proj]
    scratch = [pltpu.VMEM((R * tq, LANES), BF16), pltpu.VMEM((R * tq, 1), F32), pltpu.VMEM((R * tq, LANES), F32)]
    if mode == "fox":
        in_specs += [pl.BlockSpec((1, 1, R, tq), lambda b, p, qi, ki: (b, p, 0, qi)),
                     pl.BlockSpec((1, 1, R, tk), lambda b, p, qi, ki: (b, p, 0, jnp.minimum(ki, (qi * tq + tq - 1) // tk)))]
        args += [cum, cum]
        scratch.append(pltpu.VMEM((R * tq, 1), F32))
    elif mode == "sel":
        Rs = sel.shape[1] // groups
        in_specs.append(pl.BlockSpec((1, Rs, tq, LANES), lambda b, p, qi, ki: (b, p, qi, 0)))
        args.append(sel)
    kern = functools.partial(_flash_kernel, mode=mode, R=R, Rs=Rs, tq=tq, tk=tk, W=W, rows_cfg=rows_cfg,
                             out_cfg=out_cfg, nk=nk)
    ow = (max(c[0] for c in out_cfg) + 1) * LANES
    return pl.pallas_call(
        kern, grid=(batch, groups, n // tq, nk), in_specs=in_specs,
        out_specs=pl.BlockSpec((1, tq, ow), lambda b, p, qi, ki: (b, qi, out_blk(p))),
        out_shape=jax.ShapeDtypeStruct((batch, n, out_width), F32), scratch_shapes=scratch,
        compiler_params=_cparams(("parallel", "parallel", "parallel", "arbitrary")), name=name)(*args)


def _kmean_kernel(k_ref, o_ref, *, nb):
    k = k_ref[0]
    km = jnp.sum(k.reshape(nb, MOBA_BLOCK, LANES), axis=1) * (1.0 / MOBA_BLOCK)
    o_ref[0, 0] = jnp.concatenate([km, jnp.zeros((LANES - nb, LANES), F32)], axis=0)


def moba_kmean(proj, batch, n):
    nb = n // MOBA_BLOCK
    pairs = MOBA_HEADS // 2
    kcol = EVEN_OFF[1] // LANES
    return pl.pallas_call(
        functools.partial(_kmean_kernel, nb=nb), grid=(batch, pairs),
        in_specs=[pl.BlockSpec((1, n, LANES), lambda b, p: (b, 0, kcol + p))],
        out_specs=pl.BlockSpec((1, 1, LANES, LANES), lambda b, p: (b, p, 0, 0)),
        out_shape=jax.ShapeDtypeStruct((batch, pairs, LANES, LANES), F32),
        compiler_params=_cparams(("parallel", "parallel")), name="moba_kmean")(proj)


def _top_lanes(x, count):
    lane = lax.broadcasted_iota(jnp.int32, x.shape, 1)
    chosen = jnp.zeros(x.shape, jnp.bool_)
    for _ in range(count):
        m = jnp.max(x, axis=-1, keepdims=True)
        idx = jnp.min(jnp.where(x == m, lane, LANES), axis=-1, keepdims=True)
        hit = lane == idx
        chosen = chosen | hit
        x = jnp.where(hit, -jnp.inf, x)
    return chosen


def _moba_gate_kernel(q_ref, km_ref, sel_ref, *, tq):
    q0 = pl.program_id(2) * tq
    lane = lax.broadcasted_iota(jnp.int32, (tq, LANES), 1)
    n_past = (q0 + lax.broadcasted_iota(jnp.int32, (tq, 1), 0)) // MOBA_BLOCK
    km = km_ref[0, 0].astype(BF16)
    q2 = q_ref[0]
    for r in range(2):
        qr = jnp.where((lane >= r * HEAD_DIM) & (lane < (r + 1) * HEAD_DIM), q2, 0.0).astype(BF16)
        gate = _nt_dot(qr, km)
        valid = lane < n_past
        top = _top_lanes(jnp.where(valid, gate, -jnp.inf), MOBA_TOPK)
        sel_ref[0, r] = ((top & valid) | (lane == n_past)).astype(F32)


def moba_gate(proj, kmean, batch, n, tq):
    pairs = MOBA_HEADS // 2
    qcol = EVEN_OFF[0] // LANES
    return pl.pallas_call(
        functools.partial(_moba_gate_kernel, tq=tq), grid=(batch, pairs, n // tq),
        in_specs=[pl.BlockSpec((1, tq, LANES), lambda b, p, qi: (b, qi, qcol + p)),
                  pl.BlockSpec((1, 1, LANES, LANES), lambda b, p, qi: (b, p, 0, 0))],
        out_specs=pl.BlockSpec((1, 2, tq, LANES), lambda b, p, qi: (b, p, qi, 0)),
        out_shape=jax.ShapeDtypeStruct((batch, MOBA_HEADS, n, LANES), F32),
        compiler_params=_cparams(("parallel", "parallel", "parallel")), name="moba_gate")(proj, kmean)


def _gelu(x):
    return 0.5 * x * (1.0 + lax.erf(x * (2.0 ** -0.5)))


def _compress_kernel(c_ref, pa_ref, pb_ref, w1a_ref, w1b_ref, w2_ref, o_ref):
    c = c_ref[0, 0]
    nch = c.shape[0]
    first = jnp.dot((c + pa_ref[0]).astype(BF16), w1a_ref[0], preferred_element_type=F32)
    second = jnp.dot((c + pb_ref[0]).astype(BF16), w1b_ref[0], preferred_element_type=F32)
    hidden = _gelu(first + pltpu.roll(second, nch - 1, axis=0))
    o_ref[0, 0] = jnp.dot(hidden.astype(BF16), w2_ref[0], preferred_element_type=F32)


def nsa_compress(chunks, pos_a, pos_b, w1a, w1b, w2):
    _, batch, nch, cw = chunks.shape
    hid = w1a.shape[2]
    return pl.pallas_call(
        _compress_kernel, grid=(2, batch),
        in_specs=[pl.BlockSpec((1, 1, nch, cw), lambda t, b: (t, b, 0, 0)),
                  pl.BlockSpec((1, 1, cw), lambda t, b: (t, 0, 0)), pl.BlockSpec((1, 1, cw), lambda t, b: (t, 0, 0)),
                  pl.BlockSpec((1, cw, hid), lambda t, b: (t, 0, 0)), pl.BlockSpec((1, cw, hid), lambda t, b: (t, 0, 0)),
                  pl.BlockSpec((1, hid, LANES), lambda t, b: (t, 0, 0))],
        out_specs=pl.BlockSpec((1, 1, nch, LANES), lambda t, b: (t, b, 0, 0)),
        out_shape=jax.ShapeDtypeStruct((2, batch, nch, LANES), F32),
        compiler_params=_cparams(("parallel", "parallel")), name="nsa_compress")(chunks, pos_a, pos_b, w1a, w1b, w2)


def _nsa_cmp_kernel(sl_ref, q_ref, kc_ref, vc_ref, oc_ref, sel_ref, q_scr, *, tq, nch, rows_cfg, out_cfg):
    q0 = pl.program_id(1) * tq
    _stack_queries(q_ref, q_scr, rows_cfg, tq)
    kc = kc_ref[0, 0].astype(BF16)
    vc = vc_ref[0, 0].astype(BF16)
    s = _nt_dot(q_scr[...], kc)
    qpos = q0 + lax.broadcasted_iota(jnp.int32, (tq, 1), 0)
    cmp_end = lax.broadcasted_iota(jnp.int32, (1, nch), 1) * NSA_CMP_STRIDE + (NSA_CMP_LEN - 1)
    mask = cmp_end <= qpos
    qf = qpos.astype(F32)
    ef = cmp_end.astype(F32)
    ci = lax.broadcasted_iota(jnp.int32, (nch, 1), 0) * NSA_CMP_STRIDE
    sj = lax.broadcasted_iota(jnp.int32, (1, LANES), 1) * NSA_SEL_BLOCK
    cover = ((ci < sj + NSA_SEL_BLOCK) & (ci + NSA_CMP_LEN > sj)).astype(BF16)
    imp = jnp.zeros((tq, LANES), F32)
    vals = []
    for r in range(4):
        slope = sl_ref[0, r:r + 1, 0:1]
        sr = jnp.where(mask, s[r * tq:(r + 1) * tq] + slope * ef - slope * qf, NEG)
        m = jnp.max(sr, axis=-1, keepdims=True)
        e = jnp.where(mask, jnp.exp(sr - m), 0.0)
        p = (e / jnp.maximum(jnp.sum(e, axis=-1, keepdims=True), 1e-30)).astype(BF16)
        vals.append(jnp.dot(p, vc, preferred_element_type=F32))
        imp = imp + jnp.dot(p, cover, preferred_element_type=F32)
    for j, blk in enumerate(_assemble_heads(vals, rows_cfg, out_cfg, tq)):
        oc_ref[0, :, j * LANES:(j + 1) * LANES] = blk
    lane = lax.broadcasted_iota(jnp.int32, (tq, LANES), 1)
    cur = qpos // NSA_SEL_BLOCK
    causal = lane <= cur
    forced = (lane == 0) | (lane == cur) | (lane == cur - 1)
    ranked = jnp.where(causal, jnp.where(forced, jnp.inf, imp), -jnp.inf)
    sel_ref[0, 0] = (_top_lanes(ranked, NSA_SEL_TOPK) & causal).astype(F32)


def nsa_cmp_select(proj, cmp, slopes, g, batch, n, tq):
    rows_cfg, out_cfg = _group_cfg(g)
    nch = cmp.shape[2]
    qcol = EVEN_OFF[3] // (2 * LANES) + g
    kern = functools.partial(_nsa_cmp_kernel, tq=tq, nch=nch, rows_cfg=rows_cfg, out_cfg=out_cfg)
    return pl.pallas_call(
        kern, grid=(batch, n // tq),
        in_specs=[pl.BlockSpec((1, 8, LANES), lambda b, qi: (0, 0, 0)),
                  pl.BlockSpec((1, tq, 2 * LANES), lambda b, qi: (b, qi, qcol)),
                  pl.BlockSpec((1, 1, nch, LANES), lambda b, qi: (0, b, 0, 0)),
                  pl.BlockSpec((1, 1, nch, LANES), lambda b, qi: (1, b, 0, 0))],
        out_specs=[pl.BlockSpec((1, tq, 2 * LANES), lambda b, qi: (b, qi, 0)),
                   pl.BlockSpec((1, 1, tq, LANES), lambda b, qi: (b, 0, qi, 0))],
        out_shape=[jax.ShapeDtypeStruct((batch, n, 2 * LANES), F32),
                   jax.ShapeDtypeStruct((batch, 1, n, LANES), F32)],
        scratch_shapes=[pltpu.VMEM((4 * tq, LANES), BF16)],
        compiler_params=_cparams(("parallel", "parallel")), name="nsa_cmp_select")(slopes, proj, cmp, cmp)


_PEER_CAND = [(i, j) for i in range(PEER_TOPK) for j in range(PEER_TOPK) if (i + 1) * (j + 1) <= PEER_TOPK]
_PEER_CAND_ROWS = _round_up(len(_PEER_CAND), 8)


def _top_rows(x, count):
    nrows = x.shape[0]
    rid = lax.broadcasted_iota(jnp.int32, x.shape, 0)
    rank = jnp.full(x.shape, float(count), F32)
    vals = []
    for t in range(count):
        m = jnp.max(x, axis=0, keepdims=True)
        idx = jnp.min(jnp.where(x == m, rid, nrows), axis=0, keepdims=True)
        hit = rid == idx
        rank = jnp.where(hit, float(t), rank)
        x = jnp.where(hit, -jnp.inf, x)
        vals.append(m)
    return vals, rank


def _peer_route_kernel(x_ref, wq_ref, sk_ref, r_ref, cand_scr, s_scr, *, tm):
    q = jnp.dot(x_ref[...].astype(BF16), wq_ref[...], preferred_element_type=F32)
    half = PEER_DKEY // 2
    tl = cand_scr.shape[1]
    for h in range(PEER_HEADS):
        s_scr[0] = _nt_dot(sk_ref[0], q[:, (2 * h) * half:(2 * h + 1) * half].astype(BF16))
        s_scr[1] = _nt_dot(sk_ref[1], q[:, (2 * h + 1) * half:(2 * h + 2) * half].astype(BF16))
        for lc in range(tm // tl):
            cols = slice(lc * tl, (lc + 1) * tl)
            s1 = s_scr[0, :, cols]
            s2 = s_scr[1, :, cols]
            v1, rank1 = _top_rows(s1, PEER_TOPK)
            v2, rank2 = _top_rows(s2, PEER_TOPK)
            cand_scr[...] = jnp.full(cand_scr.shape, -jnp.inf, F32)
            for c, (i, j) in enumerate(_PEER_CAND):
                cand_scr[c:c + 1, :] = v1[i] + v2[j]
            top, crank = _top_rows(cand_scr[...], PEER_TOPK)
            picked = crank < float(PEER_TOPK)
            zsum = jnp.zeros((1, tl), F32)
            for t in range(PEER_TOPK):
                zsum = zsum + jnp.exp(top[t] - top[0])
            cnt_of_key = jnp.zeros(s1.shape, F32)
            start = 0
            for i in range(PEER_TOPK):
                width = PEER_TOPK // (i + 1)
                cnt_i = jnp.sum(jnp.where(picked[start:start + width], 1.0, 0.0), axis=0, keepdims=True)
                cnt_of_key = jnp.where(rank1 == float(i), cnt_i, cnt_of_key)
                start += width
            r_ref[h, 0, :, cols] = jnp.exp(s1 - v1[0]) / zsum
            r_ref[h, 1, :, cols] = cnt_of_key
            r_ref[h, 2, :, cols] = jnp.exp(s2 - v2[0])
            r_ref[h, 3, :, cols] = rank2


def peer_route(x, wq, subkeys, tm):
    t = x.shape[0]
    return pl.pallas_call(
        functools.partial(_peer_route_kernel, tm=tm), grid=(t // tm,),
        in_specs=[pl.BlockSpec((tm, D_MODEL), lambda i: (i, 0)),
                  pl.BlockSpec((D_MODEL, PEER_HEADS * PEER_DKEY), lambda i: (0, 0)),
                  pl.BlockSpec((2, PEER_NKEYS, PEER_DKEY // 2), lambda i: (0, 0, 0))],
        out_specs=pl.BlockSpec((PEER_HEADS, 4, PEER_NKEYS, tm), lambda i: (0, 0, 0, i)),
        out_shape=jax.ShapeDtypeStruct((PEER_HEADS, 4, PEER_NKEYS, t), F32),
        scratch_shapes=[pltpu.VMEM((_PEER_CAND_ROWS, min(tm, LANES)), F32), pltpu.VMEM((2, PEER_NKEYS, tm), F32)],
        compiler_params=_cparams(("parallel",)), name="peer_route")(x, wq, subkeys)


def _peer_expert_kernel(x_ref, r_ref, u_ref, vt_ref, g_ref, b_ref, y_ref, xb_scr, acc_scr, g_scr, act_scr, *,
                        tm, na, nchunks):
    j = pl.program_id(1)

    @pl.when(j == 0)
    def _init():
        xb_scr[...] = x_ref[...].astype(BF16)
        acc_scr[...] = jnp.zeros(acc_scr.shape, F32)

    act_scr[...] = _gelu(_nt_dot(u_ref[...], xb_scr[...]))
    tl = min(tm, LANES)
    for ai in range(na):
        a = j * na + ai
        rows = slice(ai * PEER_NKEYS, (ai + 1) * PEER_NKEYS)
        e1 = [r_ref[h, 0, pl.ds(a, 1), :] for h in range(PEER_HEADS)]
        cnt = [r_ref[h, 1, pl.ds(a, 1), :] for h in range(PEER_HEADS)]
        for lc in range(tm // tl):
            cols = slice(lc * tl, (lc + 1) * tl)
            w = jnp.zeros((PEER_NKEYS, tl), F32)
            for h in range(PEER_HEADS):
                w = w + jnp.where(r_ref[h, 3, :, cols] < cnt[h][:, cols], e1[h][:, cols] * r_ref[h, 2, :, cols], 0.0)
            g_scr[rows, cols] = (w * act_scr[rows, cols]).astype(BF16)
    acc_scr[...] += jnp.dot(vt_ref[...], g_scr[...], preferred_element_type=F32)

    @pl.when(j == nchunks - 1)
    def _fin():
        y_ref[...] = _layer_norm_rows(ALPHA * x_ref[...] + acc_scr[...].T, g_ref[...], b_ref[...])


def peer_experts(x, route, u, vt, g, b, tm, na):
    t = x.shape[0]
    ce = na * PEER_NKEYS
    nchunks = PEER_NKEYS // na
    kern = functools.partial(_peer_expert_kernel, tm=tm, na=na, nchunks=nchunks)
    return pl.pallas_call(
        kern, grid=(t // tm, nchunks),
        in_specs=[pl.BlockSpec((tm, D_MODEL), lambda i, j: (i, 0)),
                  pl.BlockSpec((PEER_HEADS, 4, PEER_NKEYS, tm), lambda i, j: (0, 0, 0, i)),
                  pl.BlockSpec((ce, D_MODEL), lambda i, j: (j, 0)),
                  pl.BlockSpec((D_MODEL, ce), lambda i, j: (0, j)),
                  pl.BlockSpec((1, D_MODEL), lambda i, j: (0, 0)), pl.BlockSpec((1, D_MODEL), lambda i, j: (0, 0))],
        out_specs=pl.BlockSpec((tm, D_MODEL), lambda i, j: (i, 0)),
        out_shape=jax.ShapeDtypeStruct((t, D_MODEL), F32),
        scratch_shapes=[pltpu.VMEM((tm, D_MODEL), BF16), pltpu.VMEM((D_MODEL, tm), F32), pltpu.VMEM((ce, tm), BF16),
                        pltpu.VMEM((ce, tm), F32)],
        compiler_params=_cparams(("parallel", "arbitrary")), name="peer_experts")(x, route, u, vt, g, b)


def _pad_cols(w, mult):
    return jnp.pad(w, ((0, 0), (0, _round_up(w.shape[1], mult) - w.shape[1])))


def _slope_table(head_slopes):
    arr = np.zeros((len(head_slopes), 8, LANES), np.float32)
    for p, sl in enumerate(head_slopes):
        for r, v in enumerate(sl):
            arr[p, r, :] = v
    return jnp.asarray(arr)


def _alibi(n):
    return [float(2.0 ** (-8.0 * (i + 1) / n)) for i in range(n)]


def _compress_weights(pos, w1, w2):
    half = NSA_CMP_LEN // 2
    eye = jnp.eye(NSA_KV_HEADS, dtype=F32)
    pos_g = jnp.broadcast_to(pos[:, :, None, :], (2, NSA_CMP_LEN, NSA_KV_HEADS, HEAD_DIM))
    pos_a = pos_g[:, :half].reshape(2, 1, -1)
    pos_b = pos_g[:, half:].reshape(2, 1, -1)
    w1r = w1.reshape(2, NSA_CMP_LEN, HEAD_DIM, NSA_CMP_HIDDEN)
    wide = jnp.einsum('tjdo,gh->tjgdho', w1r, eye)
    cw = half * NSA_KV_HEADS * HEAD_DIM
    w1a = wide[:, :half].reshape(2, cw, NSA_KV_HEADS * NSA_CMP_HIDDEN).astype(BF16)
    w1b = wide[:, half:].reshape(2, cw, NSA_KV_HEADS * NSA_CMP_HIDDEN).astype(BF16)
    w2bd = jnp.einsum('tod,gh->tgohd', w2, eye).reshape(2, NSA_KV_HEADS * NSA_CMP_HIDDEN,
                                                        NSA_KV_HEADS * HEAD_DIM).astype(BF16)
    return pos_a, pos_b, w1a, w1b, w2bd


def _moba_prompt(proj3, batch, n):
    pairs = MOBA_HEADS // 2
    rows_cfg, out_cfg = _pair_cfg()
    sl = _alibi(MOBA_HEADS)
    slopes = _slope_table([[sl[2 * p], sl[2 * p + 1]] for p in range(pairs)])
    kmean = moba_kmean(proj3, batch, n)
    sel = moba_gate(proj3, kmean, batch, n, tq=min(n, 512))
    q0, k0, v0 = (EVEN_OFF[i] // LANES for i in range(3))
    return flash_attention("sel", proj3, batch=batch, n=n, groups=pairs,
                           q_blk=lambda p: q0 + p, k_blk=lambda p: k0 + p, v_blk=lambda p: v0 + p,
                           rows_cfg=rows_cfg, out_cfg=out_cfg, out_width=MOBA_HEADS * HEAD_DIM, out_blk=lambda p: p,
                           tq=min(n, 512), tk=min(n, 512), W=MOBA_BLOCK, slopes=slopes, sel=sel, name="moba_attn")


def _nsa_prompt(proj3, batch, n, pos, w1, w2):
    sl = _alibi(NSA_HEADS)
    kv = proj3[:, :, EVEN_OFF[4]:EVEN_OFF[6]]
    chunks = jnp.stack([kv[:, :, :LANES], kv[:, :, LANES:]], 0).reshape(2, batch, n // NSA_CMP_STRIDE, -1)
    cmp = nsa_compress(chunks, *_compress_weights(pos, w1, w2))
    outs = [[], [], []]
    for g in range(NSA_KV_HEADS):
        rows_cfg, out_cfg = _group_cfg(g)
        slopes = _slope_table([sl[4 * g:4 * g + 4]])
        oc, sel = nsa_cmp_select(proj3, cmp, slopes, g, batch, n, tq=min(n, 256))
        common = dict(batch=batch, n=n, groups=1, q_blk=lambda p, g=g: EVEN_OFF[3] // (2 * LANES) + g,
                      rows_cfg=rows_cfg, out_cfg=out_cfg, out_width=2 * LANES, out_blk=lambda p: 0, slopes=slopes)
        osel = flash_attention("sel", proj3, k_blk=lambda p: EVEN_OFF[6] // LANES, v_blk=lambda p: EVEN_OFF[7] // LANES,
                               tq=min(n, 256), tk=min(n, 512), W=NSA_SEL_BLOCK, sel=sel, name="nsa_sel_attn", **common)
        owin = flash_attention("win", proj3, k_blk=lambda p: EVEN_OFF[8] // LANES, v_blk=lambda p: EVEN_OFF[9] // LANES,
                               tq=NSA_WINDOW, tk=NSA_WINDOW, name="nsa_win_attn", **common)
        for lst, o in zip(outs, (oc, osel, owin)):
            lst.append(o)
    return [jnp.concatenate(lst, -1) for lst in outs]


def _fox_prompt(proj3, cum, batch, n):
    pairs = FOX_HEADS // 2
    rows_cfg, out_cfg = _pair_cfg()
    q0, k0, v0 = (ODD_OFF[i] // LANES for i in range(3))
    cum_rows = cum.swapaxes(1, 2).reshape(batch, pairs, 2, n)
    return flash_attention("fox", proj3, batch=batch, n=n, groups=pairs,
                           q_blk=lambda p: q0 + p, k_blk=lambda p: k0 + p, v_blk=lambda p: v0 + p,
                           rows_cfg=rows_cfg, out_cfg=out_cfg, out_width=FOX_HEADS * HEAD_DIM, out_blk=lambda p: p,
                           tq=min(n, 512), tk=min(n, 512), cum=cum_rows, name="fox_attn")


def _peer_layer(h, lw, tm):
    route = peer_route(h, lw['peer_wq'], lw['peer_sk'], tm)
    return peer_experts(h, route, lw['peer_u'], lw['peer_vt'], lw['ln_ffn_g'], lw['ln_ffn_b'], tm, na=8)


def _prompt_trunk(x, lws):
    batch, n, d = x.shape
    t = batch * n
    h = x.reshape(t, d)
    tm = 512
    moba_rows, nsa_rows, win_rows, fox_rows, logf_rows = [], [], [], [], []
    for layer, lw in enumerate(lws):
        proj = matmul(h, lw['w_in'], tm, 256)
        proj3 = proj.reshape(batch, n, -1)
        if layer % 2 == 0:
            oa = _moba_prompt(proj3, batch, n)
            oc, osel, owin = _nsa_prompt(proj3, batch, n, lw['cmp_pos'], lw['cmp_w1'], lw['cmp_w2'])
            hw = MOBA_HEADS * HEAD_DIM
            h = out_proj_even(oa.reshape(t, hw), oc.reshape(t, hw), osel.reshape(t, hw), owin.reshape(t, hw), proj, h,
                              lw['w_out'], lw['ln_mix_g'], lw['ln_mix_b'], tm=256)
            moba_rows.append(proj3[:, :, EVEN_OFF[1]:EVEN_OFF[3]].reshape(batch, n, 2, MOBA_HEADS, HEAD_DIM))
            nsa_rows.append(proj3[:, :, EVEN_OFF[4]:EVEN_OFF[8]].reshape(batch, n, 4, NSA_KV_HEADS, HEAD_DIM))
            keep = min(NSA_WINDOW, n)
            win_rows.append(proj3[:, n - keep:, EVEN_OFF[8]:EVEN_OFF[10]].reshape(batch, keep, 2, NSA_KV_HEADS, HEAD_DIM))
        else:
            logf = jax.nn.log_sigmoid(proj3[:, :, ODD_OFF[3]:ODD_OFF[4]] + lw['b_forget'])
            cum = jnp.cumsum(logf, axis=1)
            o = _fox_prompt(proj3, cum, batch, n)
            h = out_proj_odd(o.reshape(t, d), h, lw['w_out'], lw['ln_mix_g'], lw['ln_mix_b'], tm=256)
            fox_rows.append(proj3[:, :, ODD_OFF[1]:ODD_OFF[3]].reshape(batch, n, 2, FOX_HEADS, HEAD_DIM))
            logf_rows.append(logf)
        h = _peer_layer(h, lw, tm)
    return (h.reshape(batch, n, d), jnp.stack(moba_rows, 1), jnp.stack(nsa_rows, 1), jnp.stack(win_rows, 1),
            jnp.stack(fox_rows, 1), jnp.stack(logf_rows, 1))


def _bf(x):
    return x.astype(BF16).astype(F32)


def _head_lanes(h):
    return slice(h * HEAD_DIM, (h + 1) * HEAD_DIM)


def _flat_pages(cache):
    return cache.reshape(cache.shape[0], cache.shape[1], -1, HEAD_DIM)


def _page_head(kv_ref, c, h, ncomp, nheads):
    return kv_ref[0, 0, pl.ds(c * nheads + h, PAGE_SIZE, stride=ncomp * nheads), :]


def _tree_sum(parts):
    while len(parts) > 1:
        parts = [parts[i] + parts[i + 1] if i + 1 < len(parts) else parts[i] for i in range(0, len(parts), 2)]
    return parts[0]


def _pick_rows(parts, rid):
    out = parts[0]
    for h in range(1, len(parts)):
        out = jnp.where(rid == h, parts[h], out)
    return out


def _rows_from_lanes(row, nheads):
    rid = lax.broadcasted_iota(jnp.int32, (nheads, HEAD_DIM), 0)
    out = jnp.zeros((nheads, HEAD_DIM), F32)
    for h in range(nheads):
        out = jnp.where(rid == h, row[:, _head_lanes(h)], out)
    return out


def _store_head_rows(o_ref, mat):
    for h in range(mat.shape[0]):
        o_ref[0, :, _head_lanes(h)] = mat[h:h + 1, :]


def _own_token_update(m_prev, l_prev, acc_prev, qm, knm, vnm):
    s_own = jnp.sum(_bf(knm) * _bf(qm), axis=1, keepdims=True)
    m_new = jnp.maximum(m_prev, s_own)
    alpha = jnp.exp(m_prev - m_new)
    p_own = jnp.exp(s_own - m_new)
    return alpha * l_prev + p_own, alpha * acc_prev + _bf(p_own) * _bf(vnm)


def _fox_dec_kernel(pt_ref, q_ref, kn_ref, vn_ref, lfn_ref, kv_ref, lf_ref, o_ref, qs_scr, m_scr, l_scr, acc_scr,
                    suf_scr, *, npages):
    nh = FOX_HEADS
    j = pl.program_id(1)
    rid = lax.broadcasted_iota(jnp.int32, (nh, HEAD_DIM), 0)
    eye = (lax.broadcasted_iota(jnp.int32, (nh, nh), 0) == lax.broadcasted_iota(jnp.int32, (nh, nh), 1))

    @pl.when(j == 0)
    def _init():
        qm = _rows_from_lanes(q_ref[0], nh) * SCALE
        for h in range(nh):
            qs_scr[h] = jnp.where(rid == h, qm, 0.0).astype(BF16)
        m_scr[...] = jnp.full(m_scr.shape, NEG, F32)
        l_scr[...] = jnp.zeros(l_scr.shape, F32)
        acc_scr[...] = jnp.zeros(acc_scr.shape, F32)
        suf_scr[...] = jnp.sum(jnp.where(eye, lfn_ref[0], 0.0), axis=1, keepdims=True)

    logf = lf_ref[0, 0]
    nrow = logf.shape[0]
    later = (lax.broadcasted_iota(jnp.int32, (nrow, nrow), 0)
             > lax.broadcasted_iota(jnp.int32, (nrow, nrow), 1)).astype(BF16)
    hi = logf.astype(BF16)
    r1 = logf - hi.astype(F32)
    mid = r1.astype(BF16)
    lo = (r1 - mid.astype(F32)).astype(BF16)
    eye_b = eye.astype(BF16)
    s = suf_scr[...] + jnp.zeros((nh, nrow), F32)
    total = jnp.zeros((nh, nrow), F32)
    for piece in (hi, mid, lo):
        piece_t = _nt_dot(eye_b, piece)
        total = total + piece_t
        s = s + jnp.dot(piece_t.astype(BF16), later, preferred_element_type=F32)
    s = s + _tree_sum([_nt_dot(qs_scr[h], _page_head(kv_ref, 0, h, 2, nh).astype(BF16)) for h in range(nh)])
    m_prev = m_scr[...]
    m_new = jnp.maximum(m_prev, jnp.max(s, axis=1, keepdims=True))
    alpha = jnp.exp(m_prev - m_new)
    p = jnp.exp(s - m_new)
    l_scr[...] = alpha * l_scr[...] + jnp.sum(p, axis=1, keepdims=True)
    pb = p.astype(BF16)
    acc = alpha * acc_scr[...]
    acc = acc + _pick_rows([jnp.dot(pb, _page_head(kv_ref, 1, h, 2, nh).astype(BF16), preferred_element_type=F32)
                            for h in range(nh)], rid)
    acc_scr[...] = acc
    m_scr[...] = m_new
    suf_scr[...] = suf_scr[...] + jnp.sum(total, axis=1, keepdims=True)

    @pl.when(j == npages - 1)
    def _fin():
        qm = _rows_from_lanes(q_ref[0], nh) * SCALE
        l_fin, acc_fin = _own_token_update(m_scr[...], l_scr[...], acc_scr[...], qm,
                                           _rows_from_lanes(kn_ref[0], nh), _rows_from_lanes(vn_ref[0], nh))
        _store_head_rows(o_ref, acc_fin / l_fin)


def fox_decode(proj3, logf_new, cache_kv, cache_logf, page_table, layer):
    batch, npages = page_table.shape
    width = FOX_HEADS * HEAD_DIM
    col = lambda c: (lambda b, j, pt: (b, 0, c))
    page = lambda b, j, pt: (pt[b, npages - 1 - j], layer, 0, 0)
    grid_spec = pltpu.PrefetchScalarGridSpec(
        num_scalar_prefetch=1, grid=(batch, npages),
        in_specs=[pl.BlockSpec((1, 1, width), col(0)), pl.BlockSpec((1, 1, width), col(1)),
                  pl.BlockSpec((1, 1, width), col(2)), pl.BlockSpec((1, 1, FOX_HEADS), lambda b, j, pt: (b, 0, 0)),
                  pl.BlockSpec((1, 1, PAGE_SIZE * 2 * FOX_HEADS, HEAD_DIM), page),
                  pl.BlockSpec((1, 1, PAGE_SIZE, FOX_HEADS), lambda b, j, pt: (pt[b, npages - 1 - j], layer, 0, 0))],
        out_specs=pl.BlockSpec((1, 1, width), lambda b, j, pt: (b, 0, 0)),
        scratch_shapes=[pltpu.VMEM((FOX_HEADS, FOX_HEADS, HEAD_DIM), BF16),
                        pltpu.VMEM((FOX_HEADS, 1), F32), pltpu.VMEM((FOX_HEADS, 1), F32),
                        pltpu.VMEM((FOX_HEADS, HEAD_DIM), F32), pltpu.VMEM((FOX_HEADS, 1), F32)])
    return pl.pallas_call(
        functools.partial(_fox_dec_kernel, npages=npages), grid_spec=grid_spec,
        out_shape=jax.ShapeDtypeStruct((batch, 1, width), F32),
        compiler_params=_cparams(("parallel", "arbitrary")), name="fox_decode")(
            page_table, proj3, proj3, proj3, logf_new, _flat_pages(cache_kv), cache_logf)


def _moba_dec_kernel(pt_ref, sl_ref, q_ref, kn_ref, vn_ref, kv_ref, o_ref, qs_scr, m_scr, l_scr, acc_scr, ks_scr, *,
                     npages):
    j = pl.program_id(1)
    blk = j // (MOBA_BLOCK // PAGE_SIZE)
    past = npages * PAGE_SIZE

    @pl.when(j == 0)
    def _init():
        m_scr[...] = jnp.full(m_scr.shape, NEG, F32)
        l_scr[...] = jnp.zeros(l_scr.shape, F32)
        acc_scr[...] = jnp.zeros(acc_scr.shape, F32)
        ks_scr[...] = jnp.zeros(ks_scr.shape, F32)

    nh = MOBA_HEADS
    rid = lax.broadcasted_iota(jnp.int32, (nh, HEAD_DIM), 0)

    @pl.when(j == 0)
    def _queries():
        qm = _rows_from_lanes(q_ref[0], nh) * SCALE
        for h in range(nh):
            qs_scr[h] = jnp.where(rid == h, qm, 0.0).astype(BF16)

    dist = (past - (j * PAGE_SIZE + lax.broadcasted_iota(jnp.int32, (1, PAGE_SIZE), 1))).astype(F32)
    s = -sl_ref[0, :, 0:1] * dist
    ksum = jnp.zeros((nh, HEAD_DIM), F32)
    parts = []
    for h in range(nh):
        k_raw = _page_head(kv_ref, 0, h, 2, nh)
        parts.append(_nt_dot(qs_scr[h], k_raw.astype(BF16)))
        ksum = jnp.where(rid == h, jnp.sum(k_raw, axis=0, keepdims=True), ksum)
    s = s + _tree_sum(parts)
    m_prev = m_scr[blk]
    m_new = jnp.maximum(m_prev, jnp.max(s, axis=1, keepdims=True))
    alpha = jnp.exp(m_prev - m_new)
    p = jnp.exp(s - m_new)
    l_scr[blk] = alpha * l_scr[blk] + jnp.sum(p, axis=1, keepdims=True)
    pb = p.astype(BF16)
    acc = alpha * acc_scr[blk]
    acc = acc + _pick_rows([jnp.dot(pb, _page_head(kv_ref, 1, h, 2, nh).astype(BF16), preferred_element_type=F32)
                            for h in range(nh)], rid)
    acc_scr[blk] = acc
    m_scr[blk] = m_new
    ks_scr[blk] = ks_scr[blk] + ksum

    @pl.when(j == npages - 1)
    def _fin():
        qm = _rows_from_lanes(q_ref[0], nh)
        nblk = m_scr.shape[0]
        gate = jnp.sum(_bf(ks_scr[...] * (1.0 / MOBA_BLOCK)) * _bf(qm), axis=2, keepdims=True)
        bid = lax.broadcasted_iota(jnp.int32, gate.shape, 0)
        chosen = jnp.zeros(gate.shape, jnp.bool_)
        for _ in range(MOBA_TOPK):
            top = jnp.max(gate, axis=0, keepdims=True)
            hit = bid == jnp.min(jnp.where(gate == top, bid, nblk), axis=0, keepdims=True)
            chosen = chosen | hit
            gate = jnp.where(hit, -jnp.inf, gate)
        knm = _rows_from_lanes(kn_ref[0], nh)
        vnm = _rows_from_lanes(vn_ref[0], nh)
        s_own = jnp.sum(_bf(knm) * _bf(qm * SCALE), axis=1, keepdims=True)
        mb = m_scr[...]
        m_all = jnp.maximum(jnp.max(jnp.where(chosen, mb, NEG), axis=0), s_own)
        w = jnp.where(chosen, jnp.exp(mb - m_all), 0.0)
        p_own = jnp.exp(s_own - m_all)
        l_all = jnp.sum(w * l_scr[...], axis=0) + p_own
        acc_all = jnp.sum(w * acc_scr[...], axis=0) + _bf(p_own) * _bf(vnm)
        _store_head_rows(o_ref, acc_all / l_all)


def moba_decode(proj3, cache_kv, page_table, layer):
    batch, npages = page_table.shape
    past = npages * PAGE_SIZE
    assert past % MOBA_BLOCK == 0 and past // MOBA_BLOCK >= MOBA_TOPK
    nblk = past // MOBA_BLOCK
    width = MOBA_HEADS * HEAD_DIM
    slopes = _slope_table([_alibi(MOBA_HEADS)])
    col = lambda c: (lambda b, j, pt: (b, 0, c))
    grid_spec = pltpu.PrefetchScalarGridSpec(
        num_scalar_prefetch=1, grid=(batch, npages),
        in_specs=[pl.BlockSpec((1, 8, LANES), lambda b, j, pt: (0, 0, 0)),
                  pl.BlockSpec((1, 1, width), col(0)), pl.BlockSpec((1, 1, width), col(1)),
                  pl.BlockSpec((1, 1, width), col(2)),
                  pl.BlockSpec((1, 1, PAGE_SIZE * 2 * MOBA_HEADS, HEAD_DIM), lambda b, j, pt: (pt[b, j], layer, 0, 0))],
        out_specs=pl.BlockSpec((1, 1, width), lambda b, j, pt: (b, 0, 0)),
        scratch_shapes=[pltpu.VMEM((MOBA_HEADS, MOBA_HEADS, HEAD_DIM), BF16),
                        pltpu.VMEM((nblk, MOBA_HEADS, 1), F32), pltpu.VMEM((nblk, MOBA_HEADS, 1), F32),
                        pltpu.VMEM((nblk, MOBA_HEADS, HEAD_DIM), F32), pltpu.VMEM((nblk, MOBA_HEADS, HEAD_DIM), F32)])
    return pl.pallas_call(
        functools.partial(_moba_dec_kernel, npages=npages), grid_spec=grid_spec,
        out_shape=jax.ShapeDtypeStruct((batch, 1, width), F32),
        compiler_params=_cparams(("parallel", "arbitrary")), name="moba_decode")(
            page_table, slopes, proj3, proj3, proj3, _flat_pages(cache_kv))


def _nsa_dec_cmp_kernel(sl_ref, q_ref, kc_ref, vc_ref, oc_ref, sel_ref, *, nch):
    past = nch * NSA_CMP_STRIDE
    q = q_ref[0]
    kc = kc_ref[0, 0]
    vc = vc_ref[0, 0]
    cmp_end = lax.broadcasted_iota(jnp.int32, (nch, 1), 0) * NSA_CMP_STRIDE + (NSA_CMP_LEN - 1)
    mask = cmp_end <= past
    dist = (past - cmp_end).astype(F32)
    ci = lax.broadcasted_iota(jnp.int32, (nch, 1), 0) * NSA_CMP_STRIDE
    sj = lax.broadcasted_iota(jnp.int32, (1, LANES), 1) * NSA_SEL_BLOCK
    cover = ((ci < sj + NSA_SEL_BLOCK) & (ci + NSA_CMP_LEN > sj)).astype(F32)
    lane = lax.broadcasted_iota(jnp.int32, (1, LANES), 1)
    own = past // NSA_SEL_BLOCK
    rep = NSA_HEADS // NSA_KV_HEADS
    for g in range(NSA_KV_HEADS):
        kg = _bf(kc[:, _head_lanes(g)])
        vg = _bf(vc[:, _head_lanes(g)])
        imp = jnp.zeros((1, LANES), F32)
        for r in range(rep):
            h = g * rep + r
            slope = sl_ref[0, h:h + 1, 0:1]
            qh = _bf(q[:, _head_lanes(h)]) * SCALE
            s = jnp.where(mask, jnp.sum(kg * qh, axis=1, keepdims=True) - slope * dist, NEG)
            m = jnp.max(s, axis=0, keepdims=True)
            e = jnp.where(mask, jnp.exp(s - m), 0.0)
            p = _bf(e / jnp.maximum(jnp.sum(e, axis=0, keepdims=True), 1e-30))
            oc_ref[0, :, _head_lanes(h)] = jnp.sum(p * vg, axis=0, keepdims=True)
            imp = imp + jnp.sum(p * cover, axis=0, keepdims=True)
        forced = (lane == 0) | (lane == own - 1)
        sel_ref[0, g:g + 1, :] = _top_lanes(jnp.where(forced, jnp.inf, imp), NSA_SEL_TOPK - 1).astype(F32)


def nsa_decode_cmp(proj3, cmp):
    batch = proj3.shape[0]
    nch = cmp.shape[2]
    assert nch * NSA_CMP_STRIDE == LANES * NSA_SEL_BLOCK
    width = NSA_HEADS * HEAD_DIM
    slopes = _slope_table([_alibi(NSA_HEADS)])
    return pl.pallas_call(
        functools.partial(_nsa_dec_cmp_kernel, nch=nch), grid=(batch,),
        in_specs=[pl.BlockSpec((1, 8, LANES), lambda b: (0, 0, 0)),
                  pl.BlockSpec((1, 1, width), lambda b: (b, 0, EVEN_OFF[3] // width)),
                  pl.BlockSpec((1, 1, nch, LANES), lambda b: (0, b, 0, 0)),
                  pl.BlockSpec((1, 1, nch, LANES), lambda b: (1, b, 0, 0))],
        out_specs=[pl.BlockSpec((1, 1, width), lambda b: (b, 0, 0)),
                   pl.BlockSpec((1, NSA_KV_HEADS, LANES), lambda b: (b, 0, 0))],
        out_shape=[jax.ShapeDtypeStruct((batch, 1, width), F32),
                   jax.ShapeDtypeStruct((batch, NSA_KV_HEADS, LANES), F32)],
        compiler_params=_cparams(("parallel",)), name="nsa_decode_cmp")(slopes, proj3, cmp, cmp)


def _nsa_dec_mix_kernel(pt_ref, sl_ref, q_ref, ksn_ref, vsn_ref, kwn_ref, vwn_ref, gl_ref, oc_ref, sel_ref, kv_ref,
                        win_ref, o_ref, qs_scr, m_scr, l_scr, acc_scr, *, npages):
    j = pl.program_id(1)
    past = npages * PAGE_SIZE
    nh = NSA_HEADS
    rep = nh // NSA_KV_HEADS
    rid = lax.broadcasted_iota(jnp.int32, (nh, HEAD_DIM), 0)
    rid1 = lax.broadcasted_iota(jnp.int32, (nh, 1), 0)
    slope = sl_ref[0, :, 0:1]

    def group_rows(x0, x1):
        return jnp.where(rid < rep, x0, x1)

    @pl.when(j == 0)
    def _init():
        qm = _rows_from_lanes(q_ref[0], nh) * SCALE
        for g in range(NSA_KV_HEADS):
            qs_scr[g] = jnp.where(rid // rep == g, qm, 0.0).astype(BF16)
        m_scr[...] = jnp.full(m_scr.shape, NEG, F32)
        l_scr[...] = jnp.zeros(l_scr.shape, F32)
        acc_scr[...] = jnp.zeros(acc_scr.shape, F32)

    pos_lane = lax.broadcasted_iota(jnp.int32, (1, PAGE_SIZE), 1)
    dist = (past - (j * PAGE_SIZE + pos_lane)).astype(F32)
    lane = lax.broadcasted_iota(jnp.int32, (1, LANES), 1)
    per_page = PAGE_SIZE // NSA_SEL_BLOCK
    s = -slope * dist
    keeps = []
    for g in range(NSA_KV_HEADS):
        selg = sel_ref[0, g:g + 1, :]
        keep = jnp.zeros((1, PAGE_SIZE), F32)
        for c in range(per_page):
            flag = jnp.sum(jnp.where(lane == j * per_page + c, selg, 0.0), axis=1, keepdims=True)
            keep = jnp.where((pos_lane >= c * NSA_SEL_BLOCK) & (pos_lane < (c + 1) * NSA_SEL_BLOCK), flag, keep)
        keeps.append(keep)
        s = s + _nt_dot(qs_scr[g], kv_ref[0, 0, :, 2, g, :].astype(BF16))
    s = jnp.where(jnp.where(rid1 < rep, keeps[0], keeps[1]) > 0.5, s, NEG)
    m_prev = m_scr[...]
    m_new = jnp.maximum(m_prev, jnp.max(s, axis=1, keepdims=True))
    alpha = jnp.exp(m_prev - m_new)
    p = jnp.exp(s - m_new)
    l_scr[...] = alpha * l_scr[...] + jnp.sum(p, axis=1, keepdims=True)
    pb = p.astype(BF16)
    acc = alpha * acc_scr[...]
    for g in range(NSA_KV_HEADS):
        res = jnp.dot(pb, kv_ref[0, 0, :, 3, g, :].astype(BF16), preferred_element_type=F32)
        acc = jnp.where(rid // rep == g, acc + res, acc)
    acc_scr[...] = acc
    m_scr[...] = m_new

    @pl.when(j == npages - 1)
    def _fin():
        qm = _rows_from_lanes(q_ref[0], nh) * SCALE
        halves = lambda ref: group_rows(ref[0][:, _head_lanes(0)], ref[0][:, _head_lanes(1)])
        l_s, acc_s = _own_token_update(m_scr[...], l_scr[...], acc_scr[...], qm, halves(ksn_ref), halves(vsn_ref))
        o_s = acc_s / l_s
        nwin = win_ref.shape[2]
        wdist = (nwin - lax.broadcasted_iota(jnp.int32, (1, nwin), 1)).astype(F32)
        s_w = -slope * wdist
        for g in range(NSA_KV_HEADS):
            s_w = s_w + _nt_dot(qs_scr[g], win_ref[0, 0, :, 0, g, :].astype(BF16))
        m_w = jnp.max(s_w, axis=1, keepdims=True)
        p_w = jnp.exp(s_w - m_w)
        l_w = jnp.sum(p_w, axis=1, keepdims=True)
        acc_w = jnp.zeros((nh, HEAD_DIM), F32)
        for g in range(NSA_KV_HEADS):
            res = jnp.dot(p_w.astype(BF16), win_ref[0, 0, :, 1, g, :].astype(BF16), preferred_element_type=F32)
            acc_w = jnp.where(rid // rep == g, res, acc_w)
        l_w, acc_w = _own_token_update(m_w, l_w, acc_w, qm, halves(kwn_ref), halves(vwn_ref))
        o_w = acc_w / l_w
        sig = jax.nn.sigmoid(gl_ref[0])
        gates = []
        for c in range(3):
            col = jnp.zeros((nh, 1), F32)
            for h in range(nh):
                col = jnp.where(rid1 == h, sig[:, 3 * h + c:3 * h + c + 1], col)
            gates.append(col)
        o_c = _rows_from_lanes(oc_ref[0], nh)
        _store_head_rows(o_ref, gates[0] * o_c + gates[1] * o_s + gates[2] * o_w)


def nsa_decode_mix(proj3, oc, sel, cache_kv, state_win, page_table, layer):
    batch, npages = page_table.shape
    nwin = state_win.shape[2]
    assert nwin == NSA_WINDOW and npages * PAGE_SIZE >= nwin
    width = NSA_HEADS * HEAD_DIM
    slopes = _slope_table([_alibi(NSA_HEADS)])
    lanes_blk = lambda off: pl.BlockSpec((1, 1, LANES), lambda b, j, pt: (b, 0, off // LANES))
    grid_spec = pltpu.PrefetchScalarGridSpec(
        num_scalar_prefetch=1, grid=(batch, npages),
        in_specs=[pl.BlockSpec((1, 8, LANES), lambda b, j, pt: (0, 0, 0)),
                  pl.BlockSpec((1, 1, width), lambda b, j, pt: (b, 0, EVEN_OFF[3] // width)),
                  lanes_blk(EVEN_OFF[6]), lanes_blk(EVEN_OFF[7]), lanes_blk(EVEN_OFF[8]), lanes_blk(EVEN_OFF[9]),
                  lanes_blk(EVEN_OFF[10]),
                  pl.BlockSpec((1, 1, width), lambda b, j, pt: (b, 0, 0)),
                  pl.BlockSpec((1, NSA_KV_HEADS, LANES), lambda b, j, pt: (b, 0, 0)),
                  pl.BlockSpec((1, 1, PAGE_SIZE, 4, NSA_KV_HEADS, HEAD_DIM), lambda b, j, pt: (pt[b, j], layer, 0, 0, 0, 0)),
                  pl.BlockSpec((1, 1, nwin, 2, NSA_KV_HEADS, HEAD_DIM), lambda b, j, pt: (b, layer, 0, 0, 0, 0))],
        out_specs=pl.BlockSpec((1, 1, width), lambda b, j, pt: (b, 0, 0)),
        scratch_shapes=[pltpu.VMEM((NSA_KV_HEADS, NSA_HEADS, HEAD_DIM), BF16),
                        pltpu.VMEM((NSA_HEADS, 1), F32), pltpu.VMEM((NSA_HEADS, 1), F32),
                        pltpu.VMEM((NSA_HEADS, HEAD_DIM), F32)])
    return pl.pallas_call(
        functools.partial(_nsa_dec_mix_kernel, npages=npages), grid_spec=grid_spec,
        out_shape=jax.ShapeDtypeStruct((batch, 1, width), F32),
        compiler_params=_cparams(("parallel", "arbitrary")), name="nsa_decode_mix")(
            page_table, slopes, proj3, proj3, proj3, proj3, proj3, proj3, oc, sel, cache_kv, state_win)


def _sample_trunk(x, lws, caches, page_table):
    cache_moba, cache_nsa, state_win, cache_fox, cache_logf = caches
    batch, n, d = x.shape
    t = batch * n
    h = x.reshape(t, d)
    assert n == 1
    npages = page_table.shape[1]
    moba_rows, nsa_rows, win_rows, fox_rows, logf_rows = [], [], [], [], []
    for layer, lw in enumerate(lws):
        i = layer // 2
        proj = matmul(h, lw['w_in'], t, 256)
        p3 = proj.reshape(batch, n, -1)
        if layer % 2 == 0:
            o_a = moba_decode(p3, cache_moba, page_table, i)
            kvc = cache_nsa[page_table, i][:, :, :, 0:2]
            chunks = kvc.transpose(3, 0, 1, 2, 4, 5).reshape(2, batch, npages * PAGE_SIZE // NSA_CMP_STRIDE, -1)
            cmp = nsa_compress(chunks, *_compress_weights(lw['cmp_pos'], lw['cmp_w1'], lw['cmp_w2']))
            o_c, sel = nsa_decode_cmp(p3, cmp)
            o_b = nsa_decode_mix(p3, o_c, sel, cache_nsa, state_win, page_table, i)
            mixed = jnp.concatenate([o_a.reshape(t, -1), o_b.reshape(t, -1)], -1)
            h = out_proj_odd(mixed, h, lw['w_out'], lw['ln_mix_g'], lw['ln_mix_b'], tm=t)
            moba_rows.append(p3[:, :, EVEN_OFF[1]:EVEN_OFF[3]].reshape(batch, n, 2, MOBA_HEADS, HEAD_DIM))
            nsa_rows.append(p3[:, :, EVEN_OFF[4]:EVEN_OFF[8]].reshape(batch, n, 4, NSA_KV_HEADS, HEAD_DIM))
            new_win = p3[:, :, EVEN_OFF[8]:EVEN_OFF[10]].reshape(batch, n, 2, NSA_KV_HEADS, HEAD_DIM)
            win_rows.append(jnp.concatenate([state_win[:, i, n:], new_win], 1))
        else:
            logf = jax.nn.log_sigmoid(p3[:, :, ODD_OFF[3]:ODD_OFF[4]] + lw['b_forget'])
            o_c = fox_decode(p3, logf, cache_fox, cache_logf, page_table, i)
            h = out_proj_odd(o_c.reshape(t, -1), h, lw['w_out'], lw['ln_mix_g'], lw['ln_mix_b'], tm=t)
            fox_rows.append(p3[:, :, ODD_OFF[1]:ODD_OFF[3]].reshape(batch, n, 2, FOX_HEADS, HEAD_DIM))
            logf_rows.append(logf)
        h = _peer_layer(h, lw, t)
    return (h.reshape(batch, n, d), jnp.stack(moba_rows, 1), jnp.stack(nsa_rows, 1), jnp.stack(win_rows, 1),
            jnp.stack(fox_rows, 1), jnp.stack(logf_rows, 1))


def _layer_weights(layer, w_in_ab, w_out_ab, nsa_cmp_pos, nsa_cmp_w1, nsa_cmp_w2, w_in_c, b_forget, w_out_c,
                   ln_mix_g, ln_mix_b, ln_ffn_g, ln_ffn_b, peer_wq, peer_subkeys, peer_u, peer_v):
    i = layer // 2
    lw = dict(ln_mix_g=ln_mix_g[layer][None], ln_mix_b=ln_mix_b[layer][None],
              ln_ffn_g=ln_ffn_g[layer][None], ln_ffn_b=ln_ffn_b[layer][None],
              peer_wq=peer_wq[layer].astype(BF16), peer_sk=peer_subkeys[layer].astype(BF16),
              peer_u=peer_u[layer].astype(BF16), peer_vt=peer_v[layer].astype(BF16).T)
    if layer % 2 == 0:
        lw.update(w_in=_pad_cols(w_in_ab[i], 256).astype(BF16), w_out=w_out_ab[i].astype(BF16),
                  cmp_pos=nsa_cmp_pos[i], cmp_w1=nsa_cmp_w1[i], cmp_w2=nsa_cmp_w2[i])
    else:
        lw.update(w_in=_pad_cols(w_in_c[i], 256).astype(BF16), w_out=w_out_c[i].astype(BF16), b_forget=b_forget[i])
    return lw


def kernel(x_prompt, x_sample, cache_moba_kv, cache_nsa_kv, state_nsa_win, cache_fox_kv, cache_fox_logf, page_table,
           w_in_ab, w_out_ab, nsa_cmp_pos, nsa_cmp_w1, nsa_cmp_w2, w_in_c, b_forget, w_out_c,
           ln_mix_g, ln_mix_b, ln_ffn_g, ln_ffn_b, peer_wq, peer_subkeys, peer_u, peer_v):
    lws = [_layer_weights(layer, w_in_ab, w_out_ab, nsa_cmp_pos, nsa_cmp_w1, nsa_cmp_w2, w_in_c, b_forget, w_out_c,
                          ln_mix_g, ln_mix_b, ln_ffn_g, ln_ffn_b, peer_wq, peer_subkeys, peer_u, peer_v)
           for layer in range(DEPTH)]
    y_p, moba_p, nsa_p, win_p, fox_p, logf_p = _prompt_trunk(x_prompt, lws)
    y_s, moba_s, nsa_s, win_s, fox_s, logf_s = _sample_trunk(
        x_sample, lws, (cache_moba_kv, cache_nsa_kv, state_nsa_win, cache_fox_kv, cache_fox_logf), page_table)
    return (y_p, y_s, moba_p, moba_s, nsa_p, nsa_s, win_p, win_s, fox_p, fox_s, logf_p, logf_s)
```

```python
import functools
import math

import numpy as np
import jax
import jax.numpy as jnp
from jax import lax
from jax.experimental import pallas as pl
from jax.experimental.pallas import tpu as pltpu

F32 = jnp.float32
BF16 = jnp.bfloat16

D_MODEL = 1024
DEPTH = 4
PAGE_SIZE = 128
HEAD_DIM = 64
LANES = 128
MOBA_HEADS = 8
MOBA_BLOCK = 256
MOBA_TOPK = 3
MOBA_QBLK = 32
NSA_HEADS = 8
NSA_KV_HEADS = 2
NSA_CMP_LEN = 32
NSA_CMP_STRIDE = 16
NSA_CMP_HIDDEN = 128
NSA_SEL_BLOCK = 64
NSA_SEL_TOPK = 16
NSA_WINDOW = 512
FOX_HEADS = 16
QBLK = 128
PEER_HEADS = 8
PEER_NKEYS = 128
PEER_EXPERTS = PEER_NKEYS * PEER_NKEYS
PEER_TOPK = 16
PEER_DKEY = 256
PEER_TBLK = 128
N_EVEN = (DEPTH + 1) // 2
N_ODD = DEPTH // 2
ALPHA = (2 * DEPTH) ** 0.25
LN_EPS = 1e-5
EVEN_COLS = (MOBA_HEADS * HEAD_DIM,) * 3 + (NSA_HEADS * HEAD_DIM,) + (NSA_KV_HEADS * HEAD_DIM,) * 6 + (NSA_HEADS * 3,)
ODD_COLS = (FOX_HEADS * HEAD_DIM,) * 3 + (FOX_HEADS,)
EVEN_OFF = tuple(int(v) for v in np.cumsum((0,) + EVEN_COLS))
ODD_OFF = tuple(int(v) for v in np.cumsum((0,) + ODD_COLS))
SCALE = HEAD_DIM ** -0.5
NEG = -1e30
VMEM_LIMIT = 56 * 1024 * 1024
FLASH_ROW_CHUNK = 512


def _cparams(sem):
    return pltpu.CompilerParams(dimension_semantics=sem, vmem_limit_bytes=VMEM_LIMIT)


def _round_up(n, m):
    return -(-n // m) * m


def _nt_dot(a, b):
    return lax.dot_general(a, b, (((1,), (1,)), ((), ())), preferred_element_type=F32)


def _mm_kernel(x_ref, w_ref, o_ref):
    o_ref[...] = jnp.dot(x_ref[...].astype(BF16), w_ref[...], preferred_element_type=F32)


def matmul(x, w, tm, tn):
    m, k = x.shape
    n = w.shape[1]
    return pl.pallas_call(
        _mm_kernel, grid=(m // tm, n // tn),
        in_specs=[pl.BlockSpec((tm, k), lambda i, j: (i, 0)), pl.BlockSpec((k, tn), lambda i, j: (0, j))],
        out_specs=pl.BlockSpec((tm, tn), lambda i, j: (i, j)),
        out_shape=jax.ShapeDtypeStruct((m, n), F32),
        compiler_params=_cparams(("parallel", "parallel")), name="proj_matmul")(x, w)


def _layer_norm_rows(z, g, b):
    mu = jnp.mean(z, axis=-1, keepdims=True)
    zc = z - mu
    var = jnp.mean(zc * zc, axis=-1, keepdims=True)
    return zc * lax.rsqrt(var + LN_EPS) * g + b


def _out_even_kernel(oa_ref, oc_ref, os_ref, ow_ref, gl_ref, x_ref, w_ref, g_ref, b_ref, y_ref):
    sig = jax.nn.sigmoid(gl_ref[...])
    width = NSA_HEADS * HEAD_DIM
    head_of_lane = lax.broadcasted_iota(jnp.int32, (1, width), 1) // HEAD_DIM

    def expand(j):
        acc = jnp.zeros((sig.shape[0], width), F32)
        for h in range(NSA_HEADS):
            acc = jnp.where(head_of_lane == h, sig[:, 3 * h + j:3 * h + j + 1], acc)
        return acc

    ob = expand(0) * oc_ref[...] + expand(1) * os_ref[...] + expand(2) * ow_ref[...]
    half = MOBA_HEADS * HEAD_DIM
    mixed = (jnp.dot(oa_ref[...].astype(BF16), w_ref[0:half, :], preferred_element_type=F32)
             + jnp.dot(ob.astype(BF16), w_ref[half:, :], preferred_element_type=F32))
    y_ref[...] = _layer_norm_rows(ALPHA * x_ref[...] + mixed, g_ref[...], b_ref[...])


def out_proj_even(oa, oc, os_, ow, proj, x, w, g, b, tm):
    m = x.shape[0]
    hw = MOBA_HEADS * HEAD_DIM
    row = lambda i: (i, 0)
    fixed = lambda i: (0, 0)
    return pl.pallas_call(
        _out_even_kernel, grid=(m // tm,),
        in_specs=[pl.BlockSpec((tm, hw), row)] * 4
        + [pl.BlockSpec((tm, LANES), lambda i: (i, EVEN_OFF[10] // LANES)),
           pl.BlockSpec((tm, D_MODEL), row), pl.BlockSpec((2 * hw, D_MODEL), fixed),
           pl.BlockSpec((1, D_MODEL), fixed), pl.BlockSpec((1, D_MODEL), fixed)],
        out_specs=pl.BlockSpec((tm, D_MODEL), row),
        out_shape=jax.ShapeDtypeStruct((m, D_MODEL), F32),
        compiler_params=_cparams(("parallel",)), name="out_proj_even")(oa, oc, os_, ow, proj, x, w, g, b)


def _out_odd_kernel(o_ref, x_ref, w_ref, g_ref, b_ref, y_ref):
    mixed = jnp.dot(o_ref[...].astype(BF16), w_ref[...], preferred_element_type=F32)
    y_ref[...] = _layer_norm_rows(ALPHA * x_ref[...] + mixed, g_ref[...], b_ref[...])


def out_proj_odd(o, x, w, g, b, tm):
    m = x.shape[0]
    row = lambda i: (i, 0)
    fixed = lambda i: (0, 0)
    return pl.pallas_call(
        _out_odd_kernel, grid=(m // tm,),
        in_specs=[pl.BlockSpec((tm, D_MODEL), row), pl.BlockSpec((tm, D_MODEL), row),
                  pl.BlockSpec((D_MODEL, D_MODEL), fixed),
                  pl.BlockSpec((1, D_MODEL), fixed), pl.BlockSpec((1, D_MODEL), fixed)],
        out_specs=pl.BlockSpec((tm, D_MODEL), row),
        out_shape=jax.ShapeDtypeStruct((m, D_MODEL), F32),
        compiler_params=_cparams(("parallel",)), name="out_proj_odd")(o, x, w, g, b)


def _stack_queries(q_ref, q_scr, rows_cfg, tq):
    lane = lax.broadcasted_iota(jnp.int32, (tq, LANES), 1)
    for r, (qb, qh, kh) in enumerate(rows_cfg):
        x = q_ref[0, :, qb * LANES:(qb + 1) * LANES]
        if qh != kh:
            x = pltpu.roll(x, HEAD_DIM, axis=1)
        x = jnp.where((lane >= kh * HEAD_DIM) & (lane < (kh + 1) * HEAD_DIM), x * SCALE, 0.0)
        q_scr[r * tq:(r + 1) * tq, :] = x.astype(BF16)


def _assemble_heads(vals, rows_cfg, out_cfg, tq):
    lane = lax.broadcasted_iota(jnp.int32, (tq, LANES), 1)
    nblk = max(ob for ob, _ in out_cfg) + 1
    blocks = []
    for j in range(nblk):
        parts = {}
        for r, (ob, oh) in enumerate(out_cfg):
            if ob != j:
                continue
            x = vals[r]
            if rows_cfg[r][2] != oh:
                x = pltpu.roll(x, HEAD_DIM, axis=1)
            parts[oh] = x
        blocks.append(jnp.where(lane < HEAD_DIM, parts[0], parts[1]))
    return blocks


def _pair_cfg():
    return [(0, 0, 0), (0, 1, 1)], [(0, 0), (0, 1)]


def _group_cfg(g):
    rows = [(r // 2, r % 2, g) for r in range(4)]
    outs = [(r // 2, r % 2) for r in range(4)]
    return rows, outs


def _flash_kernel(*refs, mode, R, Rs, tq, tk, W, rows_cfg, out_cfg, nk):
    if mode == "fox":
        q_ref, k_ref, v_ref, cq_ref, ck_ref, o_ref, q_scr, m_scr, acc_scr, cq_scr = refs
    elif mode == "sel":
        sl_ref, q_ref, k_ref, v_ref, sel_ref, o_ref, q_scr, m_scr, acc_scr = refs
    else:
        sl_ref, q_ref, k_ref, v_ref, o_ref, q_scr, m_scr, acc_scr = refs
    qi = pl.program_id(2)
    ki = pl.program_id(3)
    q0 = qi * tq

    @pl.when(ki == 0)
    def _init():
        _stack_queries(q_ref, q_scr, rows_cfg, tq)
        m_scr[...] = jnp.full(m_scr.shape, NEG, F32)
        acc_scr[...] = jnp.zeros(acc_scr.shape, F32)
        if mode == "fox":
            eye = (lax.broadcasted_iota(jnp.int32, (tq, tq), 0) == lax.broadcasted_iota(jnp.int32, (tq, tq), 1))
            for r in range(R):
                row = cq_ref[0, 0, r:r + 1, :]
                cq_scr[r * tq:(r + 1) * tq, :] = jnp.sum(jnp.where(eye, row, 0.0), axis=1, keepdims=True)

    if mode == "win":
        kt = qi - 1 + ki
        active = kt >= 0
        k0 = kt * tk
    else:
        active = ki * tk <= q0 + tq - 1
        k0 = ki * tk

    ch = min(tq, FLASH_ROW_CHUNK)

    @pl.when(active)
    def _step():
        k = k_ref[0].astype(BF16)
        vraw = v_ref[0]
        lane_v = lax.broadcasted_iota(jnp.int32, (tk, LANES), 1)
        v_half = [jnp.where((lane_v >= kh * HEAD_DIM) & (lane_v < (kh + 1) * HEAD_DIM), vraw, 1.0).astype(BF16)
                  for kh in range(2)]
        kpos = k0 + lax.broadcasted_iota(jnp.int32, (1, tk), 1)
        if mode == "sel":
            jrow = lax.broadcasted_iota(jnp.int32, (LANES, 1), 0) * W
            expand = ((kpos >= jrow) & (kpos < jrow + W)).astype(BF16)
        if mode != "fox":
            kf = kpos.astype(F32)
        for r in range(R):
            v = v_half[rows_cfg[r][2]]
            if mode == "fox":
                ck = ck_ref[0, 0, r:r + 1, :]
            else:
                slope = sl_ref[0, r:r + 1, 0:1]
                kbias = slope * kf

            def chunk(c, carry, r=r, v=v):
                off = c * ch
                rows = pl.ds(r * tq + off, ch)
                sr = _nt_dot(q_scr[rows, :], k)
                qpos = q0 + off + lax.broadcasted_iota(jnp.int32, (ch, 1), 0)
                if mode == "fox":
                    sr = sr + cq_scr[rows, :] - ck
                else:
                    sr = sr + kbias - slope * qpos.astype(F32)
                if mode == "win":
                    d = qpos - kpos
                    mask = (d >= 0) & (d <= NSA_WINDOW)
                else:
                    mask = kpos <= qpos
                if mode == "sel":
                    picked = jnp.dot(sel_ref[0, r if Rs > 1 else 0, pl.ds(off, ch), :].astype(BF16), expand,
                                     preferred_element_type=F32)
                    mask = mask & (picked > 0.5)
                sr = jnp.where(mask, sr, NEG)
                m_prev = m_scr[rows, :]
                m_new = jnp.maximum(m_prev, jnp.max(sr, axis=-1, keepdims=True))
                p = jnp.exp(sr - m_new)
                acc_scr[rows, :] = (jnp.exp(m_prev - m_new) * acc_scr[rows, :]
                                    + jnp.dot(p.astype(BF16), v, preferred_element_type=F32))
                m_scr[rows, :] = m_new
                return carry

            for c in range(tq // ch):
                chunk(c, 0)

    @pl.when(ki == nk - 1)
    def _fin():
        vals = []
        for r in range(R):
            acc = acc_scr[r * tq:(r + 1) * tq]
            vals.append(acc / pltpu.roll(acc, HEAD_DIM, axis=1))
        for j, blk in enumerate(_assemble_heads(vals, rows_cfg, out_cfg, tq)):
            o_ref[0, :, j * LANES:(j + 1) * LANES] = blk


def flash_attention(mode, proj, *, batch, n, groups, q_blk, k_blk, v_blk, rows_cfg, out_cfg, out_width, out_blk,
                    tq, tk, W=None, slopes=None, sel=None, cum=None, name="flash"):
    R = len(rows_cfg)
    qw = (max(c[0] for c in rows_cfg) + 1) * LANES
    if mode == "win":
        assert tq == tk == NSA_WINDOW
        nk = 2
        kmap = lambda col: (lambda b, p, qi, ki: (b, jnp.maximum(qi - 1 + ki, 0), col(p)))
    else:
        nk = n // tk
        kmap = lambda col: (lambda b, p, qi, ki: (b, jnp.minimum(ki, (qi * tq + tq - 1) // tk), col(p)))
    in_specs, args = [], []
    Rs = 0
    if mode != "fox":
        in_specs.append(pl.BlockSpec((1, 8, LANES), lambda b, p, qi, ki: (p, 0, 0)))
        args.append(slopes)
    in_specs += [pl.BlockSpec((1, tq, qw), lambda b, p, qi, ki: (b, qi, q_blk(p))),
                 pl.BlockSpec((1, tk, LANES), kmap(k_blk)), pl.BlockSpec((1, tk, LANES), kmap(v_blk))]
    args += [proj, proj, proj]
    scratch = [pltpu.VMEM((R * tq, LANES), BF16), pltpu.VMEM((R * tq, 1), F32), pltpu.VMEM((R * tq, LANES), F32)]
    if mode == "fox":
        in_specs += [pl.BlockSpec((1, 1, R, tq), lambda b, p, qi, ki: (b, p, 0, qi)),
                     pl.BlockSpec((1, 1, R, tk), lambda b, p, qi, ki: (b, p, 0, jnp.minimum(ki, (qi * tq + tq - 1) // tk)))]
        args += [cum, cum]
        scratch.append(pltpu.VMEM((R * tq, 1), F32))
    elif mode == "sel":
        Rs = sel.shape[1] // groups
        in_specs.append(pl.BlockSpec((1, Rs, tq, LANES), lambda b, p, qi, ki: (b, p, qi, 0)))
        args.append(sel)
    kern = functools.partial(_flash_kernel, mode=mode, R=R, Rs=Rs, tq=tq, tk=tk, W=W, rows_cfg=rows_cfg,
                             out_cfg=out_cfg, nk=nk)
    ow = (max(c[0] for c in out_cfg) + 1) * LANES
    return pl.pallas_call(
        kern, grid=(batch, groups, n // tq, nk), in_specs=in_specs,
        out_specs=pl.BlockSpec((1, tq, ow), lambda b, p, qi, ki: (b, qi, out_blk(p))),
        out_shape=jax.ShapeDtypeStruct((batch, n, out_width), F32), scratch_shapes=scratch,
        compiler_params=_cparams(("parallel", "parallel", "parallel", "arbitrary")), name=name)(*args)


def _kmean_kernel(k_ref, o_ref, *, nb):
    k = k_ref[0]
    km = jnp.sum(k.reshape(nb, MOBA_BLOCK, LANES), axis=1) * (1.0 / MOBA_BLOCK)
    o_ref[0, 0] = jnp.concatenate([km, jnp.zeros((LANES - nb, LANES), F32)], axis=0)


def moba_kmean(proj, batch, n):
    nb = n // MOBA_BLOCK
    pairs = MOBA_HEADS // 2
    kcol = EVEN_OFF[1] // LANES
    return pl.pallas_call(
        functools.partial(_kmean_kernel, nb=nb), grid=(batch, pairs),
        in_specs=[pl.BlockSpec((1, n, LANES), lambda b, p: (b, 0, kcol + p))],
        out_specs=pl.BlockSpec((1, 1, LANES, LANES), lambda b, p: (b, p, 0, 0)),
        out_shape=jax.ShapeDtypeStruct((batch, pairs, LANES, LANES), F32),
        compiler_params=_cparams(("parallel", "parallel")), name="moba_kmean")(proj)


def _top_lanes(x, count):
    lane = lax.broadcasted_iota(jnp.int32, x.shape, 1)
    chosen = jnp.zeros(x.shape, jnp.bool_)
    for _ in range(count):
        m = jnp.max(x, axis=-1, keepdims=True)
        idx = jnp.min(jnp.where(x == m, lane, LANES), axis=-1, keepdims=True)
        hit = lane == idx
        chosen = chosen | hit
        x = jnp.where(hit, -jnp.inf, x)
    return chosen


def _moba_gate_kernel(q_ref, km_ref, sel_ref, *, tq):
    q0 = pl.program_id(2) * tq
    lane = lax.broadcasted_iota(jnp.int32, (tq, LANES), 1)
    n_past = (q0 + lax.broadcasted_iota(jnp.int32, (tq, 1), 0)) // MOBA_BLOCK
    km = km_ref[0, 0].astype(BF16)
    q2 = q_ref[0]
    for r in range(2):
        qr = jnp.where((lane >= r * HEAD_DIM) & (lane < (r + 1) * HEAD_DIM), q2, 0.0).astype(BF16)
        gate = _nt_dot(qr, km)
        valid = lane < n_past
        top = _top_lanes(jnp.where(valid, gate, -jnp.inf), MOBA_TOPK)
        sel_ref[0, r] = ((top & valid) | (lane == n_past)).astype(F32)


def moba_gate(proj, kmean, batch, n, tq):
    pairs = MOBA_HEADS // 2
    qcol = EVEN_OFF[0] // LANES
    return pl.pallas_call(
        functools.partial(_moba_gate_kernel, tq=tq), grid=(batch, pairs, n // tq),
        in_specs=[pl.BlockSpec((1, tq, LANES), lambda b, p, qi: (b, qi, qcol + p)),
                  pl.BlockSpec((1, 1, LANES, LANES), lambda b, p, qi: (b, p, 0, 0))],
        out_specs=pl.BlockSpec((1, 2, tq, LANES), lambda b, p, qi: (b, p, qi, 0)),
        out_shape=jax.ShapeDtypeStruct((batch, MOBA_HEADS, n, LANES), F32),
        compiler_params=_cparams(("parallel", "parallel", "parallel")), name="moba_gate")(proj, kmean)


def _gelu(x):
    return 0.5 * x * (1.0 + lax.erf(x * (2.0 ** -0.5)))


def _compress_kernel(c_ref, pa_ref, pb_ref, w1a_ref, w1b_ref, w2_ref, o_ref):
    c = c_ref[0, 0]
    nch = c.shape[0]
    first = jnp.dot((c + pa_ref[0]).astype(BF16), w1a_ref[0], preferred_element_type=F32)
    second = jnp.dot((c + pb_ref[0]).astype(BF16), w1b_ref[0], preferred_element_type=F32)
    hidden = _gelu(first + pltpu.roll(second, nch - 1, axis=0))
    o_ref[0, 0] = jnp.dot(hidden.astype(BF16), w2_ref[0], preferred_element_type=F32)


def nsa_compress(chunks, pos_a, pos_b, w1a, w1b, w2):
    _, batch, nch, cw = chunks.shape
    hid = w1a.shape[2]
    return pl.pallas_call(
        _compress_kernel, grid=(2, batch),
        in_specs=[pl.BlockSpec((1, 1, nch, cw), lambda t, b: (t, b, 0, 0)),
                  pl.BlockSpec((1, 1, cw), lambda t, b: (t, 0, 0)), pl.BlockSpec((1, 1, cw), lambda t, b: (t, 0, 0)),
                  pl.BlockSpec((1, cw, hid), lambda t, b: (t, 0, 0)), pl.BlockSpec((1, cw, hid), lambda t, b: (t, 0, 0)),
                  pl.BlockSpec((1, hid, LANES), lambda t, b: (t, 0, 0))],
        out_specs=pl.BlockSpec((1, 1, nch, LANES), lambda t, b: (t, b, 0, 0)),
        out_shape=jax.ShapeDtypeStruct((2, batch, nch, LANES), F32),
        compiler_params=_cparams(("parallel", "parallel")), name="nsa_compress")(chunks, pos_a, pos_b, w1a, w1b, w2)


def _nsa_cmp_kernel(sl_ref, q_ref, kc_ref, vc_ref, oc_ref, sel_ref, q_scr, *, tq, nch, rows_cfg, out_cfg):
    q0 = pl.program_id(1) * tq
    _stack_queries(q_ref, q_scr, rows_cfg, tq)
    kc = kc_ref[0, 0].astype(BF16)
    vc = vc_ref[0, 0].astype(BF16)
    s = _nt_dot(q_scr[...], kc)
    qpos = q0 + lax.broadcasted_iota(jnp.int32, (tq, 1), 0)
    cmp_end = lax.broadcasted_iota(jnp.int32, (1, nch), 1) * NSA_CMP_STRIDE + (NSA_CMP_LEN - 1)
    mask = cmp_end <= qpos
    qf = qpos.astype(F32)
    ef = cmp_end.astype(F32)
    ci = lax.broadcasted_iota(jnp.int32, (nch, 1), 0) * NSA_CMP_STRIDE
    sj = lax.broadcasted_iota(jnp.int32, (1, LANES), 1) * NSA_SEL_BLOCK
    cover = ((ci < sj + NSA_SEL_BLOCK) & (ci + NSA_CMP_LEN > sj)).astype(BF16)
    imp = jnp.zeros((tq, LANES), F32)
    vals = []
    for r in range(4):
        slope = sl_ref[0, r:r + 1, 0:1]
        sr = jnp.where(mask, s[r * tq:(r + 1) * tq] + slope * ef - slope * qf, NEG)
        m = jnp.max(sr, axis=-1, keepdims=True)
        e = jnp.where(mask, jnp.exp(sr - m), 0.0)
        p = (e / jnp.maximum(jnp.sum(e, axis=-1, keepdims=True), 1e-30)).astype(BF16)
        vals.append(jnp.dot(p, vc, preferred_element_type=F32))
        imp = imp + jnp.dot(p, cover, preferred_element_type=F32)
    for j, blk in enumerate(_assemble_heads(vals, rows_cfg, out_cfg, tq)):
        oc_ref[0, :, j * LANES:(j + 1) * LANES] = blk
    lane = lax.broadcasted_iota(jnp.int32, (tq, LANES), 1)
    cur = qpos // NSA_SEL_BLOCK
    causal = lane <= cur
    forced = (lane == 0) | (lane == cur) | (lane == cur - 1)
    ranked = jnp.where(causal, jnp.where(forced, jnp.inf, imp), -jnp.inf)
    sel_ref[0, 0] = (_top_lanes(ranked, NSA_SEL_TOPK) & causal).astype(F32)


def nsa_cmp_select(proj, cmp, slopes, g, batch, n, tq):
    rows_cfg, out_cfg = _group_cfg(g)
    nch = cmp.shape[2]
    qcol = EVEN_OFF[3] // (2 * LANES) + g
    kern = functools.partial(_nsa_cmp_kernel, tq=tq, nch=nch, rows_cfg=rows_cfg, out_cfg=out_cfg)
    return pl.pallas_call(
        kern, grid=(batch, n // tq),
        in_specs=[pl.BlockSpec((1, 8, LANES), lambda b, qi: (0, 0, 0)),
                  pl.BlockSpec((1, tq, 2 * LANES), lambda b, qi: (b, qi, qcol)),
                  pl.BlockSpec((1, 1, nch, LANES), lambda b, qi: (0, b, 0, 0)),
                  pl.BlockSpec((1, 1, nch, LANES), lambda b, qi: (1, b, 0, 0))],
        out_specs=[pl.BlockSpec((1, tq, 2 * LANES), lambda b, qi: (b, qi, 0)),
                   pl.BlockSpec((1, 1, tq, LANES), lambda b, qi: (b, 0, qi, 0))],
        out_shape=[jax.ShapeDtypeStruct((batch, n, 2 * LANES), F32),
                   jax.ShapeDtypeStruct((batch, 1, n, LANES), F32)],
        scratch_shapes=[pltpu.VMEM((4 * tq, LANES), BF16)],
        compiler_params=_cparams(("parallel", "parallel")), name="nsa_cmp_select")(slopes, proj, cmp, cmp)


_PEER_CAND = [(i, j) for i in range(PEER_TOPK) for j in range(PEER_TOPK) if (i + 1) * (j + 1) <= PEER_TOPK]
_PEER_CAND_ROWS = _round_up(len(_PEER_CAND), 8)


def _top_rows(x, count):
    nrows = x.shape[0]
    rid = lax.broadcasted_iota(jnp.int32, x.shape, 0)
    rank = jnp.full(x.shape, float(count), F32)
    vals = []
    for t in range(count):
        m = jnp.max(x, axis=0, keepdims=True)
        idx = jnp.min(jnp.where(x == m, rid, nrows), axis=0, keepdims=True)
        hit = rid == idx
        rank = jnp.where(hit, float(t), rank)
        x = jnp.where(hit, -jnp.inf, x)
        vals.append(m)
    return vals, rank


def _peer_route_kernel(x_ref, wq_ref, sk_ref, r_ref, cand_scr, s_scr, *, tm):
    q = jnp.dot(x_ref[...].astype(BF16), wq_ref[...], preferred_element_type=F32)
    half = PEER_DKEY // 2
    tl = cand_scr.shape[1]
    for h in range(PEER_HEADS):
        s_scr[0] = _nt_dot(sk_ref[0], q[:, (2 * h) * half:(2 * h + 1) * half].astype(BF16))
        s_scr[1] = _nt_dot(sk_ref[1], q[:, (2 * h + 1) * half:(2 * h + 2) * half].astype(BF16))
        for lc in range(tm // tl):
            cols = slice(lc * tl, (lc + 1) * tl)
            s1 = s_scr[0, :, cols]
            s2 = s_scr[1, :, cols]
            v1, rank1 = _top_rows(s1, PEER_TOPK)
            v2, rank2 = _top_rows(s2, PEER_TOPK)
            cand_scr[...] = jnp.full(cand_scr.shape, -jnp.inf, F32)
            for c, (i, j) in enumerate(_PEER_CAND):
                cand_scr[c:c + 1, :] = v1[i] + v2[j]
            top, crank = _top_rows(cand_scr[...], PEER_TOPK)
            picked = crank < float(PEER_TOPK)
            zsum = jnp.zeros((1, tl), F32)
            for t in range(PEER_TOPK):
                zsum = zsum + jnp.exp(top[t] - top[0])
            cnt_of_key = jnp.zeros(s1.shape, F32)
            start = 0
            for i in range(PEER_TOPK):
                width = PEER_TOPK // (i + 1)
                cnt_i = jnp.sum(jnp.where(picked[start:start + width], 1.0, 0.0), axis=0, keepdims=True)
                cnt_of_key = jnp.where(rank1 == float(i), cnt_i, cnt_of_key)
                start += width
            r_ref[h, 0, :, cols] = jnp.exp(s1 - v1[0]) / zsum
            r_ref[h, 1, :, cols] = cnt_of_key
            r_ref[h, 2, :, cols] = jnp.exp(s2 - v2[0])
            r_ref[h, 3, :, cols] = rank2


def peer_route(x, wq, subkeys, tm):
    t = x.shape[0]
    return pl.pallas_call(
        functools.partial(_peer_route_kernel, tm=tm), grid=(t // tm,),
        in_specs=[pl.BlockSpec((tm, D_MODEL), lambda i: (i, 0)),
                  pl.BlockSpec((D_MODEL, PEER_HEADS * PEER_DKEY), lambda i: (0, 0)),
                  pl.BlockSpec((2, PEER_NKEYS, PEER_DKEY // 2), lambda i: (0, 0, 0))],
        out_specs=pl.BlockSpec((PEER_HEADS, 4, PEER_NKEYS, tm), lambda i: (0, 0, 0, i)),
        out_shape=jax.ShapeDtypeStruct((PEER_HEADS, 4, PEER_NKEYS, t), F32),
        scratch_shapes=[pltpu.VMEM((_PEER_CAND_ROWS, min(tm, LANES)), F32), pltpu.VMEM((2, PEER_NKEYS, tm), F32)],
        compiler_params=_cparams(("parallel",)), name="peer_route")(x, wq, subkeys)


def _peer_expert_kernel(x_ref, r_ref, u_ref, vt_ref, g_ref, b_ref, y_ref, xb_scr, acc_scr, g_scr, act_scr, *,
                        tm, na, nchunks):
    j = pl.program_id(1)

    @pl.when(j == 0)
    def _init():
        xb_scr[...] = x_ref[...].astype(BF16)
        acc_scr[...] = jnp.zeros(acc_scr.shape, F32)

    act_scr[...] = _gelu(_nt_dot(u_ref[...], xb_scr[...]))
    tl = min(tm, LANES)
    for ai in range(na):
        a = j * na + ai
        rows = slice(ai * PEER_NKEYS, (ai + 1) * PEER_NKEYS)
        e1 = [r_ref[h, 0, pl.ds(a, 1), :] for h in range(PEER_HEADS)]
        cnt = [r_ref[h, 1, pl.ds(a, 1), :] for h in range(PEER_HEADS)]
        for lc in range(tm // tl):
            cols = slice(lc * tl, (lc + 1) * tl)
            w = jnp.zeros((PEER_NKEYS, tl), F32)
            for h in range(PEER_HEADS):
                w = w + jnp.where(r_ref[h, 3, :, cols] < cnt[h][:, cols], e1[h][:, cols] * r_ref[h, 2, :, cols], 0.0)
            g_scr[rows, cols] = (w * act_scr[rows, cols]).astype(BF16)
    acc_scr[...] += jnp.dot(vt_ref[...], g_scr[...], preferred_element_type=F32)

    @pl.when(j == nchunks - 1)
    def _fin():
        y_ref[...] = _layer_norm_rows(ALPHA * x_ref[...] + acc_scr[...].T, g_ref[...], b_ref[...])


def peer_experts(x, route, u, vt, g, b, tm, na):
    t = x.shape[0]
    ce = na * PEER_NKEYS
    nchunks = PEER_NKEYS // na
    kern = functools.partial(_peer_expert_kernel, tm=tm, na=na, nchunks=nchunks)
    return pl.pallas_call(
        kern, grid=(t // tm, nchunks),
        in_specs=[pl.BlockSpec((tm, D_MODEL), lambda i, j: (i, 0)),
                  pl.BlockSpec((PEER_HEADS, 4, PEER_NKEYS, tm), lambda i, j: (0, 0, 0, i)),
                  pl.BlockSpec((ce, D_MODEL), lambda i, j: (j, 0)),
                  pl.BlockSpec((D_MODEL, ce), lambda i, j: (0, j)),
                  pl.BlockSpec((1, D_MODEL), lambda i, j: (0, 0)), pl.BlockSpec((1, D_MODEL), lambda i, j: (0, 0))],
        out_specs=pl.BlockSpec((tm, D_MODEL), lambda i, j: (i, 0)),
        out_shape=jax.ShapeDtypeStruct((t, D_MODEL), F32),
        scratch_shapes=[pltpu.VMEM((tm, D_MODEL), BF16), pltpu.VMEM((D_MODEL, tm), F32), pltpu.VMEM((ce, tm), BF16),
                        pltpu.VMEM((ce, tm), F32)],
        compiler_params=_cparams(("parallel", "arbitrary")), name="peer_experts")(x, route, u, vt, g, b)


def _pad_cols(w, mult):
    return jnp.pad(w, ((0, 0), (0, _round_up(w.shape[1], mult) - w.shape[1])))


def _slope_table(head_slopes):
    arr = np.zeros((len(head_slopes), 8, LANES), np.float32)
    for p, sl in enumerate(head_slopes):
        for r, v in enumerate(sl):
            arr[p, r, :] = v
    return jnp.asarray(arr)


def _alibi(n):
    return [float(2.0 ** (-8.0 * (i + 1) / n)) for i in range(n)]


def _compress_weights(pos, w1, w2):
    half = NSA_CMP_LEN // 2
    eye = jnp.eye(NSA_KV_HEADS, dtype=F32)
    pos_g = jnp.broadcast_to(pos[:, :, None, :], (2, NSA_CMP_LEN, NSA_KV_HEADS, HEAD_DIM))
    pos_a = pos_g[:, :half].reshape(2, 1, -1)
    pos_b = pos_g[:, half:].reshape(2, 1, -1)
    w1r = w1.reshape(2, NSA_CMP_LEN, HEAD_DIM, NSA_CMP_HIDDEN)
    wide = jnp.einsum('tjdo,gh->tjgdho', w1r, eye)
    cw = half * NSA_KV_HEADS * HEAD_DIM
    w1a = wide[:, :half].reshape(2, cw, NSA_KV_HEADS * NSA_CMP_HIDDEN).astype(BF16)
    w1b = wide[:, half:].reshape(2, cw, NSA_KV_HEADS * NSA_CMP_HIDDEN).astype(BF16)
    w2bd = jnp.einsum('tod,gh->tgohd', w2, eye).reshape(2, NSA_KV_HEADS * NSA_CMP_HIDDEN,
                                                        NSA_KV_HEADS * HEAD_DIM).astype(BF16)
    return pos_a, pos_b, w1a, w1b, w2bd


def _moba_prompt(proj3, batch, n):
    pairs = MOBA_HEADS // 2
    rows_cfg, out_cfg = _pair_cfg()
    sl = _alibi(MOBA_HEADS)
    slopes = _slope_table([[sl[2 * p], sl[2 * p + 1]] for p in range(pairs)])
    kmean = moba_kmean(proj3, batch, n)
    sel = moba_gate(proj3, kmean, batch, n, tq=min(n, 512))
    q0, k0, v0 = (EVEN_OFF[i] // LANES for i in range(3))
    return flash_attention("sel", proj3, batch=batch, n=n, groups=pairs,
                           q_blk=lambda p: q0 + p, k_blk=lambda p: k0 + p, v_blk=lambda p: v0 + p,
                           rows_cfg=rows_cfg, out_cfg=out_cfg, out_width=MOBA_HEADS * HEAD_DIM, out_blk=lambda p: p,
                           tq=min(n, 512), tk=min(n, 512), W=MOBA_BLOCK, slopes=slopes, sel=sel, name="moba_attn")


def _nsa_prompt(proj3, batch, n, pos, w1, w2):
    sl = _alibi(NSA_HEADS)
    kv = proj3[:, :, EVEN_OFF[4]:EVEN_OFF[6]]
    chunks = jnp.stack([kv[:, :, :LANES], kv[:, :, LANES:]], 0).reshape(2, batch, n // NSA_CMP_STRIDE, -1)
    cmp = nsa_compress(chunks, *_compress_weights(pos, w1, w2))
    outs = [[], [], []]
    for g in range(NSA_KV_HEADS):
        rows_cfg, out_cfg = _group_cfg(g)
        slopes = _slope_table([sl[4 * g:4 * g + 4]])
        oc, sel = nsa_cmp_select(proj3, cmp, slopes, g, batch, n, tq=min(n, 256))
        common = dict(batch=batch, n=n, groups=1, q_blk=lambda p, g=g: EVEN_OFF[3] // (2 * LANES) + g,
                      rows_cfg=rows_cfg, out_cfg=out_cfg, out_width=2 * LANES, out_blk=lambda p: 0, slopes=slopes)
        osel = flash_attention("sel", proj3, k_blk=lambda p: EVEN_OFF[6] // LANES, v_blk=lambda p: EVEN_OFF[7] // LANES,
                               tq=min(n, 256), tk=min(n, 512), W=NSA_SEL_BLOCK, sel=sel, name="nsa_sel_attn", **common)
        owin = flash_attention("win", proj3, k_blk=lambda p: EVEN_OFF[8] // LANES, v_blk=lambda p: EVEN_OFF[9] // LANES,
                               tq=NSA_WINDOW, tk=NSA_WINDOW, name="nsa_win_attn", **common)
        for lst, o in zip(outs, (oc, osel, owin)):
            lst.append(o)
    return [jnp.concatenate(lst, -1) for lst in outs]


def _fox_prompt(proj3, cum, batch, n):
    pairs = FOX_HEADS // 2
    rows_cfg, out_cfg = _pair_cfg()
    q0, k0, v0 = (ODD_OFF[i] // LANES for i in range(3))
    cum_rows = cum.swapaxes(1, 2).reshape(batch, pairs, 2, n)
    return flash_attention("fox", proj3, batch=batch, n=n, groups=pairs,
                           q_blk=lambda p: q0 + p, k_blk=lambda p: k0 + p, v_blk=lambda p: v0 + p,
                           rows_cfg=rows_cfg, out_cfg=out_cfg, out_width=FOX_HEADS * HEAD_DIM, out_blk=lambda p: p,
                           tq=min(n, 512), tk=min(n, 512), cum=cum_rows, name="fox_attn")


def _peer_layer(h, lw, tm):
    route = peer_route(h, lw['peer_wq'], lw['peer_sk'], tm)
    return peer_experts(h, route, lw['peer_u'], lw['peer_vt'], lw['ln_ffn_g'], lw['ln_ffn_b'], tm, na=8)


def _prompt_trunk(x, lws):
    batch, n, d = x.shape
    t = batch * n
    h = x.reshape(t, d)
    tm = 512
    moba_rows, nsa_rows, win_rows, fox_rows, logf_rows = [], [], [], [], []
    for layer, lw in enumerate(lws):
        proj = matmul(h, lw['w_in'], tm, 256)
        proj3 = proj.reshape(batch, n, -1)
        if layer % 2 == 0:
            oa = _moba_prompt(proj3, batch, n)
            oc, osel, owin = _nsa_prompt(proj3, batch, n, lw['cmp_pos'], lw['cmp_w1'], lw['cmp_w2'])
            hw = MOBA_HEADS * HEAD_DIM
            h = out_proj_even(oa.reshape(t, hw), oc.reshape(t, hw), osel.reshape(t, hw), owin.reshape(t, hw), proj, h,
                              lw['w_out'], lw['ln_mix_g'], lw['ln_mix_b'], tm=256)
            moba_rows.append(proj3[:, :, EVEN_OFF[1]:EVEN_OFF[3]].reshape(batch, n, 2, MOBA_HEADS, HEAD_DIM))
            nsa_rows.append(proj3[:, :, EVEN_OFF[4]:EVEN_OFF[8]].reshape(batch, n, 4, NSA_KV_HEADS, HEAD_DIM))
            keep = min(NSA_WINDOW, n)
            win_rows.append(proj3[:, n - keep:, EVEN_OFF[8]:EVEN_OFF[10]].reshape(batch, keep, 2, NSA_KV_HEADS, HEAD_DIM))
        else:
            logf = jax.nn.log_sigmoid(proj3[:, :, ODD_OFF[3]:ODD_OFF[4]] + lw['b_forget'])
            cum = jnp.cumsum(logf, axis=1)
            o = _fox_prompt(proj3, cum, batch, n)
            h = out_proj_odd(o.reshape(t, d), h, lw['w_out'], lw['ln_mix_g'], lw['ln_mix_b'], tm=256)
            fox_rows.append(proj3[:, :, ODD_OFF[1]:ODD_OFF[3]].reshape(batch, n, 2, FOX_HEADS, HEAD_DIM))
            logf_rows.append(logf)
        h = _peer_layer(h, lw, tm)
    return (h.reshape(batch, n, d), jnp.stack(moba_rows, 1), jnp.stack(nsa_rows, 1), jnp.stack(win_rows, 1),
            jnp.stack(fox_rows, 1), jnp.stack(logf_rows, 1))


def _bf(x):
    return x.astype(BF16).astype(F32)


def _head_lanes(h):
    return slice(h * HEAD_DIM, (h + 1) * HEAD_DIM)


def _flat_pages(cache):
    return cache.reshape(cache.shape[0], cache.shape[1], cache.shape[2], -1)


def _page_head(kv_ref, c, h, ncomp, nheads):
    del ncomp
    return kv_ref[0, 0, :, (c * nheads + h) * HEAD_DIM:(c * nheads + h + 1) * HEAD_DIM]


def _tree_sum(parts):
    while len(parts) > 1:
        parts = [parts[i] + parts[i + 1] if i + 1 < len(parts) else parts[i] for i in range(0, len(parts), 2)]
    return parts[0]


def _pair_rows(row, nheads, masked):
    rid = lax.broadcasted_iota(jnp.int32, (nheads, LANES), 0)
    lane = lax.broadcasted_iota(jnp.int32, (nheads, LANES), 1)
    out = jnp.zeros((nheads, LANES), F32)
    for p in range(nheads // 2):
        out = jnp.where(rid // 2 == p, row[:, p * LANES:(p + 1) * LANES], out)
    if masked:
        out = jnp.where((lane >= HEAD_DIM) == (rid % 2 == 1), out, 0.0)
    return out


def _pair_block(kv_ref, c, p, nheads):
    start = (c * nheads + 2 * p) * HEAD_DIM
    return kv_ref[0, 0, :, start:start + LANES]


def _pick_pairs(parts, rid):
    out = parts[0]
    for p in range(1, len(parts)):
        out = jnp.where(rid // 2 == p, parts[p], out)
    return out


def _store_head_halves(o_ref, mat):
    for h in range(mat.shape[0]):
        half = (h % 2) * HEAD_DIM
        o_ref[0, :, _head_lanes(h)] = mat[h:h + 1, half:half + HEAD_DIM]


def _rows_from_lanes(row, nheads):
    rid = lax.broadcasted_iota(jnp.int32, (nheads, HEAD_DIM), 0)
    out = jnp.zeros((nheads, HEAD_DIM), F32)
    for h in range(nheads):
        out = jnp.where(rid == h, row[:, _head_lanes(h)], out)
    return out


def _store_head_rows(o_ref, mat):
    for h in range(mat.shape[0]):
        o_ref[0, :, _head_lanes(h)] = mat[h:h + 1, :]


def _own_token_update(m_prev, l_prev, acc_prev, qm, knm, vnm):
    s_own = jnp.sum(_bf(knm) * _bf(qm), axis=1, keepdims=True)
    m_new = jnp.maximum(m_prev, s_own)
    alpha = jnp.exp(m_prev - m_new)
    p_own = jnp.exp(s_own - m_new)
    return alpha * l_prev + p_own, alpha * acc_prev + _bf(p_own) * _bf(vnm)


def _fox_dec_kernel(pt_ref, q_ref, kn_ref, vn_ref, lfn_ref, kv_ref, lf_ref, o_ref, qs_scr, m_scr, l_scr, acc_scr,
                    suf_scr, *, npages):
    nh = FOX_HEADS
    j = pl.program_id(1)
    rid = lax.broadcasted_iota(jnp.int32, (nh, LANES), 0)
    eye = (lax.broadcasted_iota(jnp.int32, (nh, nh), 0) == lax.broadcasted_iota(jnp.int32, (nh, nh), 1))

    @pl.when(j == 0)
    def _init():
        qw = _pair_rows(q_ref[0], nh, True) * SCALE
        for p in range(nh // 2):
            qs_scr[p] = jnp.where(rid // 2 == p, qw, 0.0).astype(BF16)
        m_scr[...] = jnp.full(m_scr.shape, NEG, F32)
        l_scr[...] = jnp.zeros(l_scr.shape, F32)
        acc_scr[...] = jnp.zeros(acc_scr.shape, F32)
        suf_scr[...] = jnp.sum(jnp.where(eye, lfn_ref[0], 0.0), axis=1, keepdims=True)

    logf = lf_ref[0, 0]
    nrow = logf.shape[0]
    later = (lax.broadcasted_iota(jnp.int32, (nrow, nrow), 0)
             > lax.broadcasted_iota(jnp.int32, (nrow, nrow), 1)).astype(BF16)
    hi = logf.astype(BF16)
    r1 = logf - hi.astype(F32)
    mid = r1.astype(BF16)
    lo = (r1 - mid.astype(F32)).astype(BF16)
    eye_b = eye.astype(BF16)
    pieces_t = [_nt_dot(eye_b, piece) for piece in (hi, mid, lo)]
    total = _tree_sum(pieces_t)
    s = suf_scr[...] + _tree_sum([jnp.dot(pt.astype(BF16), later, preferred_element_type=F32) for pt in pieces_t])
    s = s + _tree_sum([_nt_dot(qs_scr[p], _pair_block(kv_ref, 0, p, nh).astype(BF16)) for p in range(nh // 2)])
    m_prev = m_scr[...]
    m_new = jnp.maximum(m_prev, jnp.max(s, axis=1, keepdims=True))
    alpha = jnp.exp(m_prev - m_new)
    p = jnp.exp(s - m_new)
    l_scr[...] = alpha * l_scr[...] + jnp.sum(p, axis=1, keepdims=True)
    pb = p.astype(BF16)
    acc = alpha * acc_scr[...]
    acc = acc + _pick_pairs([jnp.dot(pb, _pair_block(kv_ref, 1, p, nh).astype(BF16), preferred_element_type=F32)
                             for p in range(nh // 2)], rid)
    acc_scr[...] = acc
    m_scr[...] = m_new
    suf_scr[...] = suf_scr[...] + jnp.sum(total, axis=1, keepdims=True)

    @pl.when(j == npages - 1)
    def _fin():
        qw = _pair_rows(q_ref[0], nh, True) * SCALE
        l_fin, acc_fin = _own_token_update(m_scr[...], l_scr[...], acc_scr[...], qw,
                                           _pair_rows(kn_ref[0], nh, False), _pair_rows(vn_ref[0], nh, False))
        _store_head_halves(o_ref, acc_fin / l_fin)


def fox_decode(proj3, logf_new, cache_kv, cache_logf, page_table, layer):
    batch, npages = page_table.shape
    width = FOX_HEADS * HEAD_DIM
    col = lambda c: (lambda b, j, pt: (b, 0, c))
    page = lambda b, j, pt: (pt[b, npages - 1 - j], layer, 0, 0)
    grid_spec = pltpu.PrefetchScalarGridSpec(
        num_scalar_prefetch=1, grid=(batch, npages),
        in_specs=[pl.BlockSpec((1, 1, width), col(0)), pl.BlockSpec((1, 1, width), col(1)),
                  pl.BlockSpec((1, 1, width), col(2)), pl.BlockSpec((1, 1, FOX_HEADS), lambda b, j, pt: (b, 0, 0)),
                  pl.BlockSpec((1, 1, PAGE_SIZE, 2 * width), page),
                  pl.BlockSpec((1, 1, PAGE_SIZE, FOX_HEADS), lambda b, j, pt: (pt[b, npages - 1 - j], layer, 0, 0))],
        out_specs=pl.BlockSpec((1, 1, width), lambda b, j, pt: (b, 0, 0)),
        scratch_shapes=[pltpu.VMEM((FOX_HEADS // 2, FOX_HEADS, LANES), BF16),
                        pltpu.VMEM((FOX_HEADS, 1), F32), pltpu.VMEM((FOX_HEADS, 1), F32),
                        pltpu.VMEM((FOX_HEADS, LANES), F32), pltpu.VMEM((FOX_HEADS, 1), F32)])
    return pl.pallas_call(
        functools.partial(_fox_dec_kernel, npages=npages), grid_spec=grid_spec,
        out_shape=jax.ShapeDtypeStruct((batch, 1, width), F32),
        compiler_params=_cparams(("parallel", "arbitrary")), name="fox_decode")(
            page_table, proj3, proj3, proj3, logf_new, _flat_pages(cache_kv), cache_logf)


def _moba_dec_kernel(pt_ref, sl_ref, q_ref, kn_ref, vn_ref, kv_ref, o_ref, qs_scr, m_scr, l_scr, acc_scr, ks_scr, *,
                     npages):
    j = pl.program_id(1)
    blk = j // (MOBA_BLOCK // PAGE_SIZE)
    past = npages * PAGE_SIZE

    @pl.when(j == 0)
    def _init():
        m_scr[...] = jnp.full(m_scr.shape, NEG, F32)
        l_scr[...] = jnp.zeros(l_scr.shape, F32)
        acc_scr[...] = jnp.zeros(acc_scr.shape, F32)
        ks_scr[...] = jnp.zeros(ks_scr.shape, F32)

    nh = MOBA_HEADS
    rid = lax.broadcasted_iota(jnp.int32, (nh, LANES), 0)

    @pl.when(j == 0)
    def _queries():
        qw = _pair_rows(q_ref[0], nh, True) * SCALE
        for p in range(nh // 2):
            qs_scr[p] = jnp.where(rid // 2 == p, qw, 0.0).astype(BF16)

    dist = (past - (j * PAGE_SIZE + lax.broadcasted_iota(jnp.int32, (1, PAGE_SIZE), 1))).astype(F32)
    s = -sl_ref[0, :, 0:1] * dist
    ksum = jnp.zeros((nh, LANES), F32)
    parts = []
    for p in range(nh // 2):
        k_raw = _pair_block(kv_ref, 0, p, nh)
        parts.append(_nt_dot(qs_scr[p], k_raw.astype(BF16)))
        ksum = jnp.where(rid // 2 == p, jnp.sum(k_raw, axis=0, keepdims=True), ksum)
    s = s + _tree_sum(parts)
    m_prev = m_scr[blk]
    m_new = jnp.maximum(m_prev, jnp.max(s, axis=1, keepdims=True))
    alpha = jnp.exp(m_prev - m_new)
    p = jnp.exp(s - m_new)
    l_scr[blk] = alpha * l_scr[blk] + jnp.sum(p, axis=1, keepdims=True)
    pb = p.astype(BF16)
    acc = alpha * acc_scr[blk]
    acc = acc + _pick_pairs([jnp.dot(pb, _pair_block(kv_ref, 1, p, nh).astype(BF16), preferred_element_type=F32)
                             for p in range(nh // 2)], rid)
    acc_scr[blk] = acc
    m_scr[blk] = m_new
    ks_scr[blk] = ks_scr[blk] + ksum

    @pl.when(j == npages - 1)
    def _fin():
        qm = _pair_rows(q_ref[0], nh, True)
        nblk = m_scr.shape[0]
        gate = jnp.sum(_bf(ks_scr[...] * (1.0 / MOBA_BLOCK)) * _bf(qm), axis=2, keepdims=True)
        bid = lax.broadcasted_iota(jnp.int32, gate.shape, 0)
        chosen = jnp.zeros(gate.shape, jnp.bool_)
        for _ in range(MOBA_TOPK):
            top = jnp.max(gate, axis=0, keepdims=True)
            hit = bid == jnp.min(jnp.where(gate == top, bid, nblk), axis=0, keepdims=True)
            chosen = chosen | hit
            gate = jnp.where(hit, -jnp.inf, gate)
        knm = _pair_rows(kn_ref[0], nh, False)
        vnm = _pair_rows(vn_ref[0], nh, False)
        s_own = jnp.sum(_bf(knm) * _bf(qm * SCALE), axis=1, keepdims=True)
        mb = m_scr[...]
        m_all = jnp.maximum(jnp.max(jnp.where(chosen, mb, NEG), axis=0), s_own)
        w = jnp.where(chosen, jnp.exp(mb - m_all), 0.0)
        p_own = jnp.exp(s_own - m_all)
        l_all = jnp.sum(w * l_scr[...], axis=0) + p_own
        acc_all = jnp.sum(w * acc_scr[...], axis=0) + _bf(p_own) * _bf(vnm)
        _store_head_halves(o_ref, acc_all / l_all)


def moba_decode(proj3, cache_kv, page_table, layer):
    batch, npages = page_table.shape
    past = npages * PAGE_SIZE
    assert past % MOBA_BLOCK == 0 and past // MOBA_BLOCK >= MOBA_TOPK
    nblk = past // MOBA_BLOCK
    width = MOBA_HEADS * HEAD_DIM
    slopes = _slope_table([_alibi(MOBA_HEADS)])
    col = lambda c: (lambda b, j, pt: (b, 0, c))
    grid_spec = pltpu.PrefetchScalarGridSpec(
        num_scalar_prefetch=1, grid=(batch, npages),
        in_specs=[pl.BlockSpec((1, 8, LANES), lambda b, j, pt: (0, 0, 0)),
                  pl.BlockSpec((1, 1, width), col(0)), pl.BlockSpec((1, 1, width), col(1)),
                  pl.BlockSpec((1, 1, width), col(2)),
                  pl.BlockSpec((1, 1, PAGE_SIZE, 2 * width), lambda b, j, pt: (pt[b, j], layer, 0, 0))],
        out_specs=pl.BlockSpec((1, 1, width), lambda b, j, pt: (b, 0, 0)),
        scratch_shapes=[pltpu.VMEM((MOBA_HEADS // 2, MOBA_HEADS, LANES), BF16),
                        pltpu.VMEM((nblk, MOBA_HEADS, 1), F32), pltpu.VMEM((nblk, MOBA_HEADS, 1), F32),
                        pltpu.VMEM((nblk, MOBA_HEADS, LANES), F32), pltpu.VMEM((nblk, MOBA_HEADS, LANES), F32)])
    return pl.pallas_call(
        functools.partial(_moba_dec_kernel, npages=npages), grid_spec=grid_spec,
        out_shape=jax.ShapeDtypeStruct((batch, 1, width), F32),
        compiler_params=_cparams(("parallel", "arbitrary")), name="moba_decode")(
            page_table, slopes, proj3, proj3, proj3, _flat_pages(cache_kv))


def _nsa_dec_cmp_kernel(sl_ref, q_ref, kc_ref, vc_ref, oc_ref, sel_ref, *, nch):
    past = nch * NSA_CMP_STRIDE
    q = q_ref[0]
    kc = kc_ref[0, 0]
    vc = vc_ref[0, 0]
    cmp_end = lax.broadcasted_iota(jnp.int32, (nch, 1), 0) * NSA_CMP_STRIDE + (NSA_CMP_LEN - 1)
    mask = cmp_end <= past
    dist = (past - cmp_end).astype(F32)
    ci = lax.broadcasted_iota(jnp.int32, (nch, 1), 0) * NSA_CMP_STRIDE
    sj = lax.broadcasted_iota(jnp.int32, (1, LANES), 1) * NSA_SEL_BLOCK
    cover = ((ci < sj + NSA_SEL_BLOCK) & (ci + NSA_CMP_LEN > sj)).astype(F32)
    lane = lax.broadcasted_iota(jnp.int32, (1, LANES), 1)
    own = past // NSA_SEL_BLOCK
    rep = NSA_HEADS // NSA_KV_HEADS
    for g in range(NSA_KV_HEADS):
        kg = _bf(kc[:, _head_lanes(g)])
        vg = _bf(vc[:, _head_lanes(g)])
        imp = jnp.zeros((1, LANES), F32)
        for r in range(rep):
            h = g * rep + r
            slope = sl_ref[0, h:h + 1, 0:1]
            qh = _bf(q[:, _head_lanes(h)]) * SCALE
            s = jnp.where(mask, jnp.sum(kg * qh, axis=1, keepdims=True) - slope * dist, NEG)
            m = jnp.max(s, axis=0, keepdims=True)
            e = jnp.where(mask, jnp.exp(s - m), 0.0)
            p = _bf(e / jnp.maximum(jnp.sum(e, axis=0, keepdims=True), 1e-30))
            oc_ref[0, :, _head_lanes(h)] = jnp.sum(p * vg, axis=0, keepdims=True)
            imp = imp + jnp.sum(p * cover, axis=0, keepdims=True)
        forced = (lane == 0) | (lane == own - 1)
        sel_ref[0, g:g + 1, :] = _top_lanes(jnp.where(forced, jnp.inf, imp), NSA_SEL_TOPK - 1).astype(F32)


def nsa_decode_cmp(proj3, cmp):
    batch = proj3.shape[0]
    nch = cmp.shape[2]
    assert nch * NSA_CMP_STRIDE == LANES * NSA_SEL_BLOCK
    width = NSA_HEADS * HEAD_DIM
    slopes = _slope_table([_alibi(NSA_HEADS)])
    return pl.pallas_call(
        functools.partial(_nsa_dec_cmp_kernel, nch=nch), grid=(batch,),
        in_specs=[pl.BlockSpec((1, 8, LANES), lambda b: (0, 0, 0)),
                  pl.BlockSpec((1, 1, width), lambda b: (b, 0, EVEN_OFF[3] // width)),
                  pl.BlockSpec((1, 1, nch, LANES), lambda b: (0, b, 0, 0)),
                  pl.BlockSpec((1, 1, nch, LANES), lambda b: (1, b, 0, 0))],
        out_specs=[pl.BlockSpec((1, 1, width), lambda b: (b, 0, 0)),
                   pl.BlockSpec((1, NSA_KV_HEADS, LANES), lambda b: (b, 0, 0))],
        out_shape=[jax.ShapeDtypeStruct((batch, 1, width), F32),
                   jax.ShapeDtypeStruct((batch, NSA_KV_HEADS, LANES), F32)],
        compiler_params=_cparams(("parallel",)), name="nsa_decode_cmp")(slopes, proj3, cmp, cmp)


def _nsa_dec_mix_kernel(pt_ref, sl_ref, q_ref, ksn_ref, vsn_ref, kwn_ref, vwn_ref, gl_ref, oc_ref, sel_ref, kv_ref,
                        win_ref, o_ref, qs_scr, m_scr, l_scr, acc_scr, *, npages):
    j = pl.program_id(1)
    past = npages * PAGE_SIZE
    nh = NSA_HEADS
    rep = nh // NSA_KV_HEADS
    rid = lax.broadcasted_iota(jnp.int32, (nh, HEAD_DIM), 0)
    rid1 = lax.broadcasted_iota(jnp.int32, (nh, 1), 0)
    slope = sl_ref[0, :, 0:1]

    def group_rows(x0, x1):
        return jnp.where(rid < rep, x0, x1)

    @pl.when(j == 0)
    def _init():
        qm = _rows_from_lanes(q_ref[0], nh) * SCALE
        for g in range(NSA_KV_HEADS):
            qs_scr[g] = jnp.where(rid // rep == g, qm, 0.0).astype(BF16)
        m_scr[...] = jnp.full(m_scr.shape, NEG, F32)
        l_scr[...] = jnp.zeros(l_scr.shape, F32)
        acc_scr[...] = jnp.zeros(acc_scr.shape, F32)

    pos_lane = lax.broadcasted_iota(jnp.int32, (1, PAGE_SIZE), 1)
    dist = (past - (j * PAGE_SIZE + pos_lane)).astype(F32)
    lane = lax.broadcasted_iota(jnp.int32, (1, LANES), 1)
    per_page = PAGE_SIZE // NSA_SEL_BLOCK
    s = -slope * dist
    keeps = []
    for g in range(NSA_KV_HEADS):
        selg = sel_ref[0, g:g + 1, :]
        keep = jnp.zeros((1, PAGE_SIZE), F32)
        for c in range(per_page):
            flag = jnp.sum(jnp.where(lane == j * per_page + c, selg, 0.0), axis=1, keepdims=True)
            keep = jnp.where((pos_lane >= c * NSA_SEL_BLOCK) & (pos_lane < (c + 1) * NSA_SEL_BLOCK), flag, keep)
        keeps.append(keep)
        s = s + _nt_dot(qs_scr[g], _page_head(kv_ref, 2, g, 4, NSA_KV_HEADS).astype(BF16))
    s = jnp.where(jnp.where(rid1 < rep, keeps[0], keeps[1]) > 0.5, s, NEG)
    m_prev = m_scr[...]
    m_new = jnp.maximum(m_prev, jnp.max(s, axis=1, keepdims=True))
    alpha = jnp.exp(m_prev - m_new)
    p = jnp.exp(s - m_new)
    l_scr[...] = alpha * l_scr[...] + jnp.sum(p, axis=1, keepdims=True)
    pb = p.astype(BF16)
    acc = alpha * acc_scr[...]
    for g in range(NSA_KV_HEADS):
        res = jnp.dot(pb, _page_head(kv_ref, 3, g, 4, NSA_KV_HEADS).astype(BF16), preferred_element_type=F32)
        acc = jnp.where(rid // rep == g, acc + res, acc)
    acc_scr[...] = acc
    m_scr[...] = m_new

    @pl.when(j == npages - 1)
    def _fin():
        qm = _rows_from_lanes(q_ref[0], nh) * SCALE
        halves = lambda ref: group_rows(ref[0][:, _head_lanes(0)], ref[0][:, _head_lanes(1)])
        l_s, acc_s = _own_token_update(m_scr[...], l_scr[...], acc_scr[...], qm, halves(ksn_ref), halves(vsn_ref))
        o_s = acc_s / l_s
        nwin = win_ref.shape[2]
        wdist = (nwin - lax.broadcasted_iota(jnp.int32, (1, nwin), 1)).astype(F32)
        s_w = -slope * wdist
        for g in range(NSA_KV_HEADS):
            s_w = s_w + _nt_dot(qs_scr[g], _page_head(win_ref, 0, g, 2, NSA_KV_HEADS).astype(BF16))
        m_w = jnp.max(s_w, axis=1, keepdims=True)
        p_w = jnp.exp(s_w - m_w)
        l_w = jnp.sum(p_w, axis=1, keepdims=True)
        acc_w = jnp.zeros((nh, HEAD_DIM), F32)
        for g in range(NSA_KV_HEADS):
            res = jnp.dot(p_w.astype(BF16), _page_head(win_ref, 1, g, 2, NSA_KV_HEADS).astype(BF16),
                          preferred_element_type=F32)
            acc_w = jnp.where(rid // rep == g, res, acc_w)
        l_w, acc_w = _own_token_update(m_w, l_w, acc_w, qm, halves(kwn_ref), halves(vwn_ref))
        o_w = acc_w / l_w
        sig = jax.nn.sigmoid(gl_ref[0])
        gates = []
        for c in range(3):
            col = jnp.zeros((nh, 1), F32)
            for h in range(nh):
                col = jnp.where(rid1 == h, sig[:, 3 * h + c:3 * h + c + 1], col)
            gates.append(col)
        o_c = _rows_from_lanes(oc_ref[0], nh)
        _store_head_rows(o_ref, gates[0] * o_c + gates[1] * o_s + gates[2] * o_w)


def nsa_decode_mix(proj3, oc, sel, cache_kv, state_win, page_table, layer):
    batch, npages = page_table.shape
    nwin = state_win.shape[2]
    assert nwin == NSA_WINDOW and npages * PAGE_SIZE >= nwin
    width = NSA_HEADS * HEAD_DIM
    slopes = _slope_table([_alibi(NSA_HEADS)])
    lanes_blk = lambda off: pl.BlockSpec((1, 1, LANES), lambda b, j, pt: (b, 0, off // LANES))
    grid_spec = pltpu.PrefetchScalarGridSpec(
        num_scalar_prefetch=1, grid=(batch, npages),
        in_specs=[pl.BlockSpec((1, 8, LANES), lambda b, j, pt: (0, 0, 0)),
                  pl.BlockSpec((1, 1, width), lambda b, j, pt: (b, 0, EVEN_OFF[3] // width)),
                  lanes_blk(EVEN_OFF[6]), lanes_blk(EVEN_OFF[7]), lanes_blk(EVEN_OFF[8]), lanes_blk(EVEN_OFF[9]),
                  lanes_blk(EVEN_OFF[10]),
                  pl.BlockSpec((1, 1, width), lambda b, j, pt: (b, 0, 0)),
                  pl.BlockSpec((1, NSA_KV_HEADS, LANES), lambda b, j, pt: (b, 0, 0)),
                  pl.BlockSpec((1, 1, PAGE_SIZE, 4 * NSA_KV_HEADS * HEAD_DIM), lambda b, j, pt: (pt[b, j], layer, 0, 0)),
                  pl.BlockSpec((1, 1, nwin, 2 * NSA_KV_HEADS * HEAD_DIM), lambda b, j, pt: (b, layer, 0, 0))],
        out_specs=pl.BlockSpec((1, 1, width), lambda b, j, pt: (b, 0, 0)),
        scratch_shapes=[pltpu.VMEM((NSA_KV_HEADS, NSA_HEADS, HEAD_DIM), BF16),
                        pltpu.VMEM((NSA_HEADS, 1), F32), pltpu.VMEM((NSA_HEADS, 1), F32),
                        pltpu.VMEM((NSA_HEADS, HEAD_DIM), F32)])
    return pl.pallas_call(
        functools.partial(_nsa_dec_mix_kernel, npages=npages), grid_spec=grid_spec,
        out_shape=jax.ShapeDtypeStruct((batch, 1, width), F32),
        compiler_params=_cparams(("parallel", "arbitrary")), name="nsa_decode_mix")(
            page_table, slopes, proj3, proj3, proj3, proj3, proj3, proj3, oc, sel, _flat_pages(cache_kv), _flat_pages(state_win))


def _sample_trunk(x, lws, caches, page_table):
    cache_moba, cache_nsa, state_win, cache_fox, cache_logf = caches
    cache_moba, cache_nsa, cache_fox = _flat_pages(cache_moba), _flat_pages(cache_nsa), _flat_pages(cache_fox)
    batch, n, d = x.shape
    t = batch * n
    h = x.reshape(t, d)
    assert n == 1
    npages = page_table.shape[1]
    moba_rows, nsa_rows, win_rows, fox_rows, logf_rows = [], [], [], [], []
    for layer, lw in enumerate(lws):
        i = layer // 2
        proj = matmul(h, lw['w_in'], t, 256)
        p3 = proj.reshape(batch, n, -1)
        if layer % 2 == 0:
            o_a = moba_decode(p3, cache_moba, page_table, i)
            kvc = cache_nsa[page_table, i][:, :, :, :2 * LANES]
            chunks = kvc.reshape(batch, npages * PAGE_SIZE, 2, LANES).transpose(2, 0, 1, 3).reshape(
                2, batch, npages * PAGE_SIZE // NSA_CMP_STRIDE, -1)
            cmp = nsa_compress(chunks, *_compress_weights(lw['cmp_pos'], lw['cmp_w1'], lw['cmp_w2']))
            o_c, sel = nsa_decode_cmp(p3, cmp)
            o_b = nsa_decode_mix(p3, o_c, sel, cache_nsa, state_win, page_table, i)
            mixed = jnp.concatenate([o_a.reshape(t, -1), o_b.reshape(t, -1)], -1)
            h = out_proj_odd(mixed, h, lw['w_out'], lw['ln_mix_g'], lw['ln_mix_b'], tm=t)
            moba_rows.append(p3[:, :, EVEN_OFF[1]:EVEN_OFF[3]].reshape(batch, n, 2, MOBA_HEADS, HEAD_DIM))
            nsa_rows.append(p3[:, :, EVEN_OFF[4]:EVEN_OFF[8]].reshape(batch, n, 4, NSA_KV_HEADS, HEAD_DIM))
            new_win = p3[:, :, EVEN_OFF[8]:EVEN_OFF[10]].reshape(batch, n, 2, NSA_KV_HEADS, HEAD_DIM)
            win_rows.append(jnp.concatenate([state_win[:, i, n:], new_win], 1))
        else:
            logf = jax.nn.log_sigmoid(p3[:, :, ODD_OFF[3]:ODD_OFF[4]] + lw['b_forget'])
            o_c = fox_decode(p3, logf, cache_fox, cache_logf, page_table, i)
            h = out_proj_odd(o_c.reshape(t, -1), h, lw['w_out'], lw['ln_mix_g'], lw['ln_mix_b'], tm=t)
            fox_rows.append(p3[:, :, ODD_OFF[1]:ODD_OFF[3]].reshape(batch, n, 2, FOX_HEADS, HEAD_DIM))
            logf_rows.append(logf)
        h = _peer_layer(h, lw, t)
    return (h.reshape(batch, n, d), jnp.stack(moba_rows, 1), jnp.stack(nsa_rows, 1), jnp.stack(win_rows, 1),
            jnp.stack(fox_rows, 1), jnp.stack(logf_rows, 1))


def _layer_weights(layer, w_in_ab, w_out_ab, nsa_cmp_pos, nsa_cmp_w1, nsa_cmp_w2, w_in_c, b_forget, w_out_c,
                   ln_mix_g, ln_mix_b, ln_ffn_g, ln_ffn_b, peer_wq, peer_subkeys, peer_u, peer_v):
    i = layer // 2
    lw = dict(ln_mix_g=ln_mix_g[layer][None], ln_mix_b=ln_mix_b[layer][None],
              ln_ffn_g=ln_ffn_g[layer][None], ln_ffn_b=ln_ffn_b[layer][None],
              peer_wq=peer_wq[layer].astype(BF16), peer_sk=peer_subkeys[layer].astype(BF16),
              peer_u=peer_u[layer].astype(BF16), peer_vt=peer_v[layer].astype(BF16).T)
    if layer % 2 == 0:
        lw.update(w_in=_pad_cols(w_in_ab[i], 256).astype(BF16), w_out=w_out_ab[i].astype(BF16),
                  cmp_pos=nsa_cmp_pos[i], cmp_w1=nsa_cmp_w1[i], cmp_w2=nsa_cmp_w2[i])
    else:
        lw.update(w_in=_pad_cols(w_in_c[i], 256).astype(BF16), w_out=w_out_c[i].astype(BF16), b_forget=b_forget[i])
    return lw


def kernel(x_prompt, x_sample, cache_moba_kv, cache_nsa_kv, state_nsa_win, cache_fox_kv, cache_fox_logf, page_table,
           w_in_ab, w_out_ab, nsa_cmp_pos, nsa_cmp_w1, nsa_cmp_w2, w_in_c, b_forget, w_out_c,
           ln_mix_g, ln_mix_b, ln_ffn_g, ln_ffn_b, peer_wq, peer_subkeys, peer_u, peer_v):
    lws = [_layer_weights(layer, w_in_ab, w_out_ab, nsa_cmp_pos, nsa_cmp_w1, nsa_cmp_w2, w_in_c, b_forget, w_out_c,
                          ln_mix_g, ln_mix_b, ln_ffn_g, ln_ffn_b, peer_wq, peer_subkeys, peer_u, peer_v)
           for layer in range(DEPTH)]
    y_p, moba_p, nsa_p, win_p, fox_p, logf_p = _prompt_trunk(x_prompt, lws)
    y_s, moba_s, nsa_s, win_s, fox_s, logf_s = _sample_trunk(
        x_sample, lws, (cache_moba_kv, cache_nsa_kv, state_nsa_win, cache_fox_kv, cache_fox_logf), page_table)
    return (y_p, y_s, moba_p, moba_s, nsa_p, nsa_s, win_p, win_s, fox_p, fox_s, logf_p, logf_s)
```

```python
import functools
import math

import numpy as np
import jax
import jax.numpy as jnp
from jax import lax
from jax.experimental import pallas as pl
from jax.experimental.pallas import tpu as pltpu

F32 = jnp.float32
BF16 = jnp.bfloat16

D_MODEL = 1024
DEPTH = 4
PAGE_SIZE = 128
HEAD_DIM = 64
LANES = 128
MOBA_HEADS = 8
MOBA_BLOCK = 256
MOBA_TOPK = 3
MOBA_QBLK = 32
NSA_HEADS = 8
NSA_KV_HEADS = 2
NSA_CMP_LEN = 32
NSA_CMP_STRIDE = 16
NSA_CMP_HIDDEN = 128
NSA_SEL_BLOCK = 64
NSA_SEL_TOPK = 16
NSA_WINDOW = 512
FOX_HEADS = 16
QBLK = 128
PEER_HEADS = 8
PEER_NKEYS = 128
PEER_EXPERTS = PEER_NKEYS * PEER_NKEYS
PEER_TOPK = 16
PEER_DKEY = 256
PEER_TBLK = 128
N_EVEN = (DEPTH + 1) // 2
N_ODD = DEPTH // 2
ALPHA = (2 * DEPTH) ** 0.25
LN_EPS = 1e-5
EVEN_COLS = (MOBA_HEADS * HEAD_DIM,) * 3 + (NSA_HEADS * HEAD_DIM,) + (NSA_KV_HEADS * HEAD_DIM,) * 6 + (NSA_HEADS * 3,)
ODD_COLS = (FOX_HEADS * HEAD_DIM,) * 3 + (FOX_HEADS,)
EVEN_OFF = tuple(int(v) for v in np.cumsum((0,) + EVEN_COLS))
ODD_OFF = tuple(int(v) for v in np.cumsum((0,) + ODD_COLS))
SCALE = HEAD_DIM ** -0.5
NEG = -1e30
VMEM_LIMIT = 56 * 1024 * 1024
FLASH_ROW_CHUNK = 512


def _cparams(sem):
    return pltpu.CompilerParams(dimension_semantics=sem, vmem_limit_bytes=VMEM_LIMIT)


def _round_up(n, m):
    return -(-n // m) * m


def _nt_dot(a, b):
    return lax.dot_general(a, b, (((1,), (1,)), ((), ())), preferred_element_type=F32)


def _mm_kernel(x_ref, w_ref, o_ref):
    o_ref[...] = jnp.dot(x_ref[...].astype(BF16), w_ref[...], preferred_element_type=F32)


def matmul(x, w, tm, tn):
    m, k = x.shape
    n = w.shape[1]
    return pl.pallas_call(
        _mm_kernel, grid=(m // tm, n // tn),
        in_specs=[pl.BlockSpec((tm, k), lambda i, j: (i, 0)), pl.BlockSpec((k, tn), lambda i, j: (0, j))],
        out_specs=pl.BlockSpec((tm, tn), lambda i, j: (i, j)),
        out_shape=jax.ShapeDtypeStruct((m, n), F32),
        compiler_params=_cparams(("parallel", "parallel")), name="proj_matmul")(x, w)


def _layer_norm_rows(z, g, b):
    mu = jnp.mean(z, axis=-1, keepdims=True)
    zc = z - mu
    var = jnp.mean(zc * zc, axis=-1, keepdims=True)
    return zc * lax.rsqrt(var + LN_EPS) * g + b


def _out_even_kernel(oa_ref, oc_ref, os_ref, ow_ref, gl_ref, x_ref, w_ref, g_ref, b_ref, y_ref):
    sig = jax.nn.sigmoid(gl_ref[...])
    width = NSA_HEADS * HEAD_DIM
    head_of_lane = lax.broadcasted_iota(jnp.int32, (1, width), 1) // HEAD_DIM

    def expand(j):
        acc = jnp.zeros((sig.shape[0], width), F32)
        for h in range(NSA_HEADS):
            acc = jnp.where(head_of_lane == h, sig[:, 3 * h + j:3 * h + j + 1], acc)
        return acc

    ob = expand(0) * oc_ref[...] + expand(1) * os_ref[...] + expand(2) * ow_ref[...]
    half = MOBA_HEADS * HEAD_DIM
    mixed = (jnp.dot(oa_ref[...].astype(BF16), w_ref[0:half, :], preferred_element_type=F32)
             + jnp.dot(ob.astype(BF16), w_ref[half:, :], preferred_element_type=F32))
    y_ref[...] = _layer_norm_rows(ALPHA * x_ref[...] + mixed, g_ref[...], b_ref[...])


def out_proj_even(oa, oc, os_, ow, proj, x, w, g, b, tm):
    m = x.shape[0]
    hw = MOBA_HEADS * HEAD_DIM
    row = lambda i: (i, 0)
    fixed = lambda i: (0, 0)
    return pl.pallas_call(
        _out_even_kernel, grid=(m // tm,),
        in_specs=[pl.BlockSpec((tm, hw), row)] * 4
        + [pl.BlockSpec((tm, LANES), lambda i: (i, EVEN_OFF[10] // LANES)),
           pl.BlockSpec((tm, D_MODEL), row), pl.BlockSpec((2 * hw, D_MODEL), fixed),
           pl.BlockSpec((1, D_MODEL), fixed), pl.BlockSpec((1, D_MODEL), fixed)],
        out_specs=pl.BlockSpec((tm, D_MODEL), row),
        out_shape=jax.ShapeDtypeStruct((m, D_MODEL), F32),
        compiler_params=_cparams(("parallel",)), name="out_proj_even")(oa, oc, os_, ow, proj, x, w, g, b)


def _out_odd_kernel(o_ref, x_ref, w_ref, g_ref, b_ref, y_ref):
    mixed = jnp.dot(o_ref[...].astype(BF16), w_ref[...], preferred_element_type=F32)
    y_ref[...] = _layer_norm_rows(ALPHA * x_ref[...] + mixed, g_ref[...], b_ref[...])


def out_proj_odd(o, x, w, g, b, tm):
    m = x.shape[0]
    row = lambda i: (i, 0)
    fixed = lambda i: (0, 0)
    return pl.pallas_call(
        _out_odd_kernel, grid=(m // tm,),
        in_specs=[pl.BlockSpec((tm, D_MODEL), row), pl.BlockSpec((tm, D_MODEL), row),
                  pl.BlockSpec((D_MODEL, D_MODEL), fixed),
                  pl.BlockSpec((1, D_MODEL), fixed), pl.BlockSpec((1, D_MODEL), fixed)],
        out_specs=pl.BlockSpec((tm, D_MODEL), row),
        out_shape=jax.ShapeDtypeStruct((m, D_MODEL), F32),
        compiler_params=_cparams(("parallel",)), name="out_proj_odd")(o, x, w, g, b)


def _stack_queries(q_ref, q_scr, rows_cfg, tq):
    lane = lax.broadcasted_iota(jnp.int32, (tq, LANES), 1)
    for r, (qb, qh, kh) in enumerate(rows_cfg):
        x = q_ref[0, :, qb * LANES:(qb + 1) * LANES]
        if qh != kh:
            x = pltpu.roll(x, HEAD_DIM, axis=1)
        x = jnp.where((lane >= kh * HEAD_DIM) & (lane < (kh + 1) * HEAD_DIM), x * SCALE, 0.0)
        q_scr[r * tq:(r + 1) * tq, :] = x.astype(BF16)


def _assemble_heads(vals, rows_cfg, out_cfg, tq):
    lane = lax.broadcasted_iota(jnp.int32, (tq, LANES), 1)
    nblk = max(ob for ob, _ in out_cfg) + 1
    blocks = []
    for j in range(nblk):
        parts = {}
        for r, (ob, oh) in enumerate(out_cfg):
            if ob != j:
                continue
            x = vals[r]
            if rows_cfg[r][2] != oh:
                x = pltpu.roll(x, HEAD_DIM, axis=1)
            parts[oh] = x
        blocks.append(jnp.where(lane < HEAD_DIM, parts[0], parts[1]))
    return blocks


def _pair_cfg():
    return [(0, 0, 0), (0, 1, 1)], [(0, 0), (0, 1)]


def _group_cfg(g):
    rows = [(r // 2, r % 2, g) for r in range(4)]
    outs = [(r // 2, r % 2) for r in range(4)]
    return rows, outs


def _flash_kernel(*refs, mode, R, Rs, tq, tk, W, rows_cfg, out_cfg, nk):
    if mode == "fox":
        q_ref, k_ref, v_ref, cq_ref, ck_ref, o_ref, q_scr, m_scr, acc_scr, cq_scr = refs
    elif mode == "sel":
        sl_ref, q_ref, k_ref, v_ref, sel_ref, o_ref, q_scr, m_scr, acc_scr = refs
    else:
        sl_ref, q_ref, k_ref, v_ref, o_ref, q_scr, m_scr, acc_scr = refs
    qi = pl.program_id(2)
    ki = pl.program_id(3)
    q0 = qi * tq

    @pl.when(ki == 0)
    def _init():
        _stack_queries(q_ref, q_scr, rows_cfg, tq)
        m_scr[...] = jnp.full(m_scr.shape, NEG, F32)
        acc_scr[...] = jnp.zeros(acc_scr.shape, F32)
        if mode == "fox":
            eye = (lax.broadcasted_iota(jnp.int32, (tq, tq), 0) == lax.broadcasted_iota(jnp.int32, (tq, tq), 1))
            for r in range(R):
                row = cq_ref[0, 0, r:r + 1, :]
                cq_scr[r * tq:(r + 1) * tq, :] = jnp.sum(jnp.where(eye, row, 0.0), axis=1, keepdims=True)

    if mode == "win":
        kt = qi - 1 + ki
        active = kt >= 0
        k0 = kt * tk
    else:
        active = ki * tk <= q0 + tq - 1
        k0 = ki * tk

    ch = min(tq, FLASH_ROW_CHUNK)

    @pl.when(active)
    def _step():
        k = k_ref[0].astype(BF16)
        vraw = v_ref[0]
        lane_v = lax.broadcasted_iota(jnp.int32, (tk, LANES), 1)
        v_half = [jnp.where((lane_v >= kh * HEAD_DIM) & (lane_v < (kh + 1) * HEAD_DIM), vraw, 1.0).astype(BF16)
                  for kh in range(2)]
        kpos = k0 + lax.broadcasted_iota(jnp.int32, (1, tk), 1)
        if mode == "sel":
            jrow = lax.broadcasted_iota(jnp.int32, (LANES, 1), 0) * W
            expand = ((kpos >= jrow) & (kpos < jrow + W)).astype(BF16)
        if mode != "fox":
            kf = kpos.astype(F32)
        for r in range(R):
            v = v_half[rows_cfg[r][2]]
            if mode == "fox":
                ck = ck_ref[0, 0, r:r + 1, :]
            else:
                slope = sl_ref[0, r:r + 1, 0:1]
                kbias = slope * kf

            def chunk(c, carry, r=r, v=v):
                off = c * ch
                rows = pl.ds(r * tq + off, ch)
                sr = _nt_dot(q_scr[rows, :], k)
                qpos = q0 + off + lax.broadcasted_iota(jnp.int32, (ch, 1), 0)
                if mode == "fox":
                    sr = sr + cq_scr[rows, :] - ck
                else:
                    sr = sr + kbias - slope * qpos.astype(F32)
                if mode == "win":
                    d = qpos - kpos
                    mask = (d >= 0) & (d <= NSA_WINDOW)
                else:
                    mask = kpos <= qpos
                if mode == "sel":
                    picked = jnp.dot(sel_ref[0, r if Rs > 1 else 0, pl.ds(off, ch), :].astype(BF16), expand,
                                     preferred_element_type=F32)
                    mask = mask & (picked > 0.5)
                sr = jnp.where(mask, sr, NEG)
                m_prev = m_scr[rows, :]
                m_new = jnp.maximum(m_prev, jnp.max(sr, axis=-1, keepdims=True))
                p = jnp.exp(sr - m_new)
                acc_scr[rows, :] = (jnp.exp(m_prev - m_new) * acc_scr[rows, :]
                                    + jnp.dot(p.astype(BF16), v, preferred_element_type=F32))
                m_scr[rows, :] = m_new
                return carry

            for c in range(tq // ch):
                chunk(c, 0)

    @pl.when(ki == nk - 1)
    def _fin():
        vals = []
        for r in range(R):
            acc = acc_scr[r * tq:(r + 1) * tq]
            vals.append(acc / pltpu.roll(acc, HEAD_DIM, axis=1))
        for j, blk in enumerate(_assemble_heads(vals, rows_cfg, out_cfg, tq)):
            o_ref[0, :, j * LANES:(j + 1) * LANES] = blk


def flash_attention(mode, proj, *, batch, n, groups, q_blk, k_blk, v_blk, rows_cfg, out_cfg, out_width, out_blk,
                    tq, tk, W=None, slopes=None, sel=None, cum=None, name="flash"):
    R = len(rows_cfg)
    qw = (max(c[0] for c in rows_cfg) + 1) * LANES
    if mode == "win":
        assert tq == tk == NSA_WINDOW
        nk = 2
        kmap = lambda col: (lambda b, p, qi, ki: (b, jnp.maximum(qi - 1 + ki, 0), col(p)))
    else:
        nk = n // tk
        kmap = lambda col: (lambda b, p, qi, ki: (b, jnp.minimum(ki, (qi * tq + tq - 1) // tk), col(p)))
    in_specs, args = [], []
    Rs = 0
    if mode != "fox":
        in_specs.append(pl.BlockSpec((1, 8, LANES), lambda b, p, qi, ki: (p, 0, 0)))
        args.append(slopes)
    in_specs += [pl.BlockSpec((1, tq, qw), lambda b, p, qi, ki: (b, qi, q_blk(p))),
                 pl.BlockSpec((1, tk, LANES), kmap(k_blk)), pl.BlockSpec((1, tk, LANES), kmap(v_blk))]
    args += [proj, proj, proj]
    scratch = [pltpu.VMEM((R * tq, LANES), BF16), pltpu.VMEM((R * tq, 1), F32), pltpu.VMEM((R * tq, LANES), F32)]
    if mode == "fox":
        in_specs += [pl.BlockSpec((1, 1, R, tq), lambda b, p, qi, ki: (b, p, 0, qi)),
                     pl.BlockSpec((1, 1, R, tk), lambda b, p, qi, ki: (b, p, 0, jnp.minimum(ki, (qi * tq + tq - 1) // tk)))]
        args += [cum, cum]
        scratch.append(pltpu.VMEM((R * tq, 1), F32))
    elif mode == "sel":
        Rs = sel.shape[1] // groups
        in_specs.append(pl.BlockSpec((1, Rs, tq, LANES), lambda b, p, qi, ki: (b, p, qi, 0)))
        args.append(sel)
    kern = functools.partial(_flash_kernel, mode=mode, R=R, Rs=Rs, tq=tq, tk=tk, W=W, rows_cfg=rows_cfg,
                             out_cfg=out_cfg, nk=nk)
    ow = (max(c[0] for c in out_cfg) + 1) * LANES
    return pl.pallas_call(
        kern, grid=(batch, groups, n // tq, nk), in_specs=in_specs,
        out_specs=pl.BlockSpec((1, tq, ow), lambda b, p, qi, ki: (b, qi, out_blk(p))),
        out_shape=jax.ShapeDtypeStruct((batch, n, out_width), F32), scratch_shapes=scratch,
        compiler_params=_cparams(("parallel", "parallel", "parallel", "arbitrary")), name=name)(*args)


def _kmean_kernel(k_ref, o_ref, *, nb):
    k = k_ref[0]
    km = jnp.sum(k.reshape(nb, MOBA_BLOCK, LANES), axis=1) * (1.0 / MOBA_BLOCK)
    o_ref[0, 0] = jnp.concatenate([km, jnp.zeros((LANES - nb, LANES), F32)], axis=0)


def moba_kmean(proj, batch, n):
    nb = n // MOBA_BLOCK
    pairs = MOBA_HEADS // 2
    kcol = EVEN_OFF[1] // LANES
    return pl.pallas_call(
        functools.partial(_kmean_kernel, nb=nb), grid=(batch, pairs),
        in_specs=[pl.BlockSpec((1, n, LANES), lambda b, p: (b, 0, kcol + p))],
        out_specs=pl.BlockSpec((1, 1, LANES, LANES), lambda b, p: (b, p, 0, 0)),
        out_shape=jax.ShapeDtypeStruct((batch, pairs, LANES, LANES), F32),
        compiler_params=_cparams(("parallel", "parallel")), name="moba_kmean")(proj)


def _top_lanes(x, count):
    lane = lax.broadcasted_iota(jnp.int32, x.shape, 1)
    chosen = jnp.zeros(x.shape, jnp.bool_)
    for _ in range(count):
        m = jnp.max(x, axis=-1, keepdims=True)
        idx = jnp.min(jnp.where(x == m, lane, LANES), axis=-1, keepdims=True)
        hit = lane == idx
        chosen = chosen | hit
        x = jnp.where(hit, -jnp.inf, x)
    return chosen


def _moba_gate_kernel(q_ref, km_ref, sel_ref, *, tq):
    q0 = pl.program_id(2) * tq
    lane = lax.broadcasted_iota(jnp.int32, (tq, LANES), 1)
    n_past = (q0 + lax.broadcasted_iota(jnp.int32, (tq, 1), 0)) // MOBA_BLOCK
    km = km_ref[0, 0].astype(BF16)
    q2 = q_ref[0]
    for r in range(2):
        qr = jnp.where((lane >= r * HEAD_DIM) & (lane < (r + 1) * HEAD_DIM), q2, 0.0).astype(BF16)
        gate = _nt_dot(qr, km)
        valid = lane < n_past
        top = _top_lanes(jnp.where(valid, gate, -jnp.inf), MOBA_TOPK)
        sel_ref[0, r] = ((top & valid) | (lane == n_past)).astype(F32)


def moba_gate(proj, kmean, batch, n, tq):
    pairs = MOBA_HEADS // 2
    qcol = EVEN_OFF[0] // LANES
    return pl.pallas_call(
        functools.partial(_moba_gate_kernel, tq=tq), grid=(batch, pairs, n // tq),
        in_specs=[pl.BlockSpec((1, tq, LANES), lambda b, p, qi: (b, qi, qcol + p)),
                  pl.BlockSpec((1, 1, LANES, LANES), lambda b, p, qi: (b, p, 0, 0))],
        out_specs=pl.BlockSpec((1, 2, tq, LANES), lambda b, p, qi: (b, p, qi, 0)),
        out_shape=jax.ShapeDtypeStruct((batch, MOBA_HEADS, n, LANES), F32),
        compiler_params=_cparams(("parallel", "parallel", "parallel")), name="moba_gate")(proj, kmean)


def _gelu(x):
    return 0.5 * x * (1.0 + lax.erf(x * (2.0 ** -0.5)))


def _compress_kernel(c_ref, pa_ref, pb_ref, w1a_ref, w1b_ref, w2_ref, o_ref):
    c = c_ref[0, 0]
    nch = c.shape[0]
    first = jnp.dot((c + pa_ref[0]).astype(BF16), w1a_ref[0], preferred_element_type=F32)
    second = jnp.dot((c + pb_ref[0]).astype(BF16), w1b_ref[0], preferred_element_type=F32)
    hidden = _gelu(first + pltpu.roll(second, nch - 1, axis=0))
    o_ref[0, 0] = jnp.dot(hidden.astype(BF16), w2_ref[0], preferred_element_type=F32)


def nsa_compress(chunks, pos_a, pos_b, w1a, w1b, w2):
    _, batch, nch, cw = chunks.shape
    hid = w1a.shape[2]
    return pl.pallas_call(
        _compress_kernel, grid=(2, batch),
        in_specs=[pl.BlockSpec((1, 1, nch, cw), lambda t, b: (t, b, 0, 0)),
                  pl.BlockSpec((1, 1, cw), lambda t, b: (t, 0, 0)), pl.BlockSpec((1, 1, cw), lambda t, b: (t, 0, 0)),
                  pl.BlockSpec((1, cw, hid), lambda t, b: (t, 0, 0)), pl.BlockSpec((1, cw, hid), lambda t, b: (t, 0, 0)),
                  pl.BlockSpec((1, hid, LANES), lambda t, b: (t, 0, 0))],
        out_specs=pl.BlockSpec((1, 1, nch, LANES), lambda t, b: (t, b, 0, 0)),
        out_shape=jax.ShapeDtypeStruct((2, batch, nch, LANES), F32),
        compiler_params=_cparams(("parallel", "parallel")), name="nsa_compress")(chunks, pos_a, pos_b, w1a, w1b, w2)


def _nsa_cmp_kernel(sl_ref, q_ref, kc_ref, vc_ref, oc_ref, sel_ref, q_scr, *, tq, nch, rows_cfg, out_cfg):
    q0 = pl.program_id(1) * tq
    _stack_queries(q_ref, q_scr, rows_cfg, tq)
    kc = kc_ref[0, 0].astype(BF16)
    vc = vc_ref[0, 0].astype(BF16)
    s = _nt_dot(q_scr[...], kc)
    qpos = q0 + lax.broadcasted_iota(jnp.int32, (tq, 1), 0)
    cmp_end = lax.broadcasted_iota(jnp.int32, (1, nch), 1) * NSA_CMP_STRIDE + (NSA_CMP_LEN - 1)
    mask = cmp_end <= qpos
    qf = qpos.astype(F32)
    ef = cmp_end.astype(F32)
    ci = lax.broadcasted_iota(jnp.int32, (nch, 1), 0) * NSA_CMP_STRIDE
    sj = lax.broadcasted_iota(jnp.int32, (1, LANES), 1) * NSA_SEL_BLOCK
    cover = ((ci < sj + NSA_SEL_BLOCK) & (ci + NSA_CMP_LEN > sj)).astype(BF16)
    imp = jnp.zeros((tq, LANES), F32)
    vals = []
    for r in range(4):
        slope = sl_ref[0, r:r + 1, 0:1]
        sr = jnp.where(mask, s[r * tq:(r + 1) * tq] + slope * ef - slope * qf, NEG)
        m = jnp.max(sr, axis=-1, keepdims=True)
        e = jnp.where(mask, jnp.exp(sr - m), 0.0)
        p = (e / jnp.maximum(jnp.sum(e, axis=-1, keepdims=True), 1e-30)).astype(BF16)
        vals.append(jnp.dot(p, vc, preferred_element_type=F32))
        imp = imp + jnp.dot(p, cover, preferred_element_type=F32)
    for j, blk in enumerate(_assemble_heads(vals, rows_cfg, out_cfg, tq)):
        oc_ref[0, :, j * LANES:(j + 1) * LANES] = blk
    lane = lax.broadcasted_iota(jnp.int32, (tq, LANES), 1)
    cur = qpos // NSA_SEL_BLOCK
    causal = lane <= cur
    forced = (lane == 0) | (lane == cur) | (lane == cur - 1)
    ranked = jnp.where(causal, jnp.where(forced, jnp.inf, imp), -jnp.inf)
    sel_ref[0, 0] = (_top_lanes(ranked, NSA_SEL_TOPK) & causal).astype(F32)


def nsa_cmp_select(proj, cmp, slopes, g, batch, n, tq):
    rows_cfg, out_cfg = _group_cfg(g)
    nch = cmp.shape[2]
    qcol = EVEN_OFF[3] // (2 * LANES) + g
    kern = functools.partial(_nsa_cmp_kernel, tq=tq, nch=nch, rows_cfg=rows_cfg, out_cfg=out_cfg)
    return pl.pallas_call(
        kern, grid=(batch, n // tq),
        in_specs=[pl.BlockSpec((1, 8, LANES), lambda b, qi: (0, 0, 0)),
                  pl.BlockSpec((1, tq, 2 * LANES), lambda b, qi: (b, qi, qcol)),
                  pl.BlockSpec((1, 1, nch, LANES), lambda b, qi: (0, b, 0, 0)),
                  pl.BlockSpec((1, 1, nch, LANES), lambda b, qi: (1, b, 0, 0))],
        out_specs=[pl.BlockSpec((1, tq, 2 * LANES), lambda b, qi: (b, qi, 0)),
                   pl.BlockSpec((1, 1, tq, LANES), lambda b, qi: (b, 0, qi, 0))],
        out_shape=[jax.ShapeDtypeStruct((batch, n, 2 * LANES), F32),
                   jax.ShapeDtypeStruct((batch, 1, n, LANES), F32)],
        scratch_shapes=[pltpu.VMEM((4 * tq, LANES), BF16)],
        compiler_params=_cparams(("parallel", "parallel")), name="nsa_cmp_select")(slopes, proj, cmp, cmp)


_PEER_CAND = [(i, j) for i in range(PEER_TOPK) for j in range(PEER_TOPK) if (i + 1) * (j + 1) <= PEER_TOPK]
_PEER_CAND_ROWS = _round_up(len(_PEER_CAND), 8)


def _top_rows(x, count):
    nrows = x.shape[0]
    rid = lax.broadcasted_iota(jnp.int32, x.shape, 0)
    rank = jnp.full(x.shape, float(count), F32)
    vals = []
    for t in range(count):
        m = jnp.max(x, axis=0, keepdims=True)
        idx = jnp.min(jnp.where(x == m, rid, nrows), axis=0, keepdims=True)
        hit = rid == idx
        rank = jnp.where(hit, float(t), rank)
        x = jnp.where(hit, -jnp.inf, x)
        vals.append(m)
    return vals, rank


def _peer_route_kernel(x_ref, wq_ref, sk_ref, r_ref, cand_scr, s_scr, *, tm):
    q = jnp.dot(x_ref[...].astype(BF16), wq_ref[...], preferred_element_type=F32)
    half = PEER_DKEY // 2
    tl = cand_scr.shape[1]
    for h in range(PEER_HEADS):
        s_scr[0] = _nt_dot(sk_ref[0], q[:, (2 * h) * half:(2 * h + 1) * half].astype(BF16))
        s_scr[1] = _nt_dot(sk_ref[1], q[:, (2 * h + 1) * half:(2 * h + 2) * half].astype(BF16))
        for lc in range(tm // tl):
            cols = slice(lc * tl, (lc + 1) * tl)
            s1 = s_scr[0, :, cols]
            s2 = s_scr[1, :, cols]
            v1, rank1 = _top_rows(s1, PEER_TOPK)
            v2, rank2 = _top_rows(s2, PEER_TOPK)
            cand_scr[...] = jnp.full(cand_scr.shape, -jnp.inf, F32)
            for c, (i, j) in enumerate(_PEER_CAND):
                cand_scr[c:c + 1, :] = v1[i] + v2[j]
            top, crank = _top_rows(cand_scr[...], PEER_TOPK)
            picked = crank < float(PEER_TOPK)
            zsum = jnp.zeros((1, tl), F32)
            for t in range(PEER_TOPK):
                zsum = zsum + jnp.exp(top[t] - top[0])
            cnt_of_key = jnp.zeros(s1.shape, F32)
            start = 0
            for i in range(PEER_TOPK):
                width = PEER_TOPK // (i + 1)
                cnt_i = jnp.sum(jnp.where(picked[start:start + width], 1.0, 0.0), axis=0, keepdims=True)
                cnt_of_key = jnp.where(rank1 == float(i), cnt_i, cnt_of_key)
                start += width
            r_ref[h, 0, :, cols] = jnp.exp(s1 - v1[0]) / zsum
            r_ref[h, 1, :, cols] = cnt_of_key
            r_ref[h, 2, :, cols] = jnp.exp(s2 - v2[0])
            r_ref[h, 3, :, cols] = rank2


def peer_route(x, wq, subkeys, tm):
    t = x.shape[0]
    return pl.pallas_call(
        functools.partial(_peer_route_kernel, tm=tm), grid=(t // tm,),
        in_specs=[pl.BlockSpec((tm, D_MODEL), lambda i: (i, 0)),
                  pl.BlockSpec((D_MODEL, PEER_HEADS * PEER_DKEY), lambda i: (0, 0)),
                  pl.BlockSpec((2, PEER_NKEYS, PEER_DKEY // 2), lambda i: (0, 0, 0))],
        out_specs=pl.BlockSpec((PEER_HEADS, 4, PEER_NKEYS, tm), lambda i: (0, 0, 0, i)),
        out_shape=jax.ShapeDtypeStruct((PEER_HEADS, 4, PEER_NKEYS, t), F32),
        scratch_shapes=[pltpu.VMEM((_PEER_CAND_ROWS, min(tm, LANES)), F32), pltpu.VMEM((2, PEER_NKEYS, tm), F32)],
        compiler_params=_cparams(("parallel",)), name="peer_route")(x, wq, subkeys)


def _peer_expert_kernel(x_ref, r_ref, u_ref, vt_ref, g_ref, b_ref, y_ref, xb_scr, acc_scr, g_scr, act_scr, *,
                        tm, na, nchunks):
    j = pl.program_id(1)

    @pl.when(j == 0)
    def _init():
        xb_scr[...] = x_ref[...].astype(BF16)
        acc_scr[...] = jnp.zeros(acc_scr.shape, F32)

    act_scr[...] = _gelu(_nt_dot(u_ref[...], xb_scr[...]))
    tl = min(tm, LANES)
    for ai in range(na):
        a = j * na + ai
        rows = slice(ai * PEER_NKEYS, (ai + 1) * PEER_NKEYS)
        e1 = [r_ref[h, 0, pl.ds(a, 1), :] for h in range(PEER_HEADS)]
        cnt = [r_ref[h, 1, pl.ds(a, 1), :] for h in range(PEER_HEADS)]
        for lc in range(tm // tl):
            cols = slice(lc * tl, (lc + 1) * tl)
            w = jnp.zeros((PEER_NKEYS, tl), F32)
            for h in range(PEER_HEADS):
                w = w + jnp.where(r_ref[h, 3, :, cols] < cnt[h][:, cols], e1[h][:, cols] * r_ref[h, 2, :, cols], 0.0)
            g_scr[rows, cols] = (w * act_scr[rows, cols]).astype(BF16)
    acc_scr[...] += jnp.dot(vt_ref[...], g_scr[...], preferred_element_type=F32)

    @pl.when(j == nchunks - 1)
    def _fin():
        y_ref[...] = _layer_norm_rows(ALPHA * x_ref[...] + acc_scr[...].T, g_ref[...], b_ref[...])


def peer_experts(x, route, u, vt, g, b, tm, na):
    t = x.shape[0]
    ce = na * PEER_NKEYS
    nchunks = PEER_NKEYS // na
    kern = functools.partial(_peer_expert_kernel, tm=tm, na=na, nchunks=nchunks)
    return pl.pallas_call(
        kern, grid=(t // tm, nchunks),
        in_specs=[pl.BlockSpec((tm, D_MODEL), lambda i, j: (i, 0)),
                  pl.BlockSpec((PEER_HEADS, 4, PEER_NKEYS, tm), lambda i, j: (0, 0, 0, i)),
                  pl.BlockSpec((ce, D_MODEL), lambda i, j: (j, 0)),
                  pl.BlockSpec((D_MODEL, ce), lambda i, j: (0, j)),
                  pl.BlockSpec((1, D_MODEL), lambda i, j: (0, 0)), pl.BlockSpec((1, D_MODEL), lambda i, j: (0, 0))],
        out_specs=pl.BlockSpec((tm, D_MODEL), lambda i, j: (i, 0)),
        out_shape=jax.ShapeDtypeStruct((t, D_MODEL), F32),
        scratch_shapes=[pltpu.VMEM((tm, D_MODEL), BF16), pltpu.VMEM((D_MODEL, tm), F32), pltpu.VMEM((ce, tm), BF16),
                        pltpu.VMEM((ce, tm), F32)],
        compiler_params=_cparams(("parallel", "arbitrary")), name="peer_experts")(x, route, u, vt, g, b)


def _pad_cols(w, mult):
    return jnp.pad(w, ((0, 0), (0, _round_up(w.shape[1], mult) - w.shape[1])))


def _slope_table(head_slopes):
    arr = np.zeros((len(head_slopes), 8, LANES), np.float32)
    for p, sl in enumerate(head_slopes):
        for r, v in enumerate(sl):
            arr[p, r, :] = v
    return jnp.asarray(arr)


def _alibi(n):
    return [float(2.0 ** (-8.0 * (i + 1) / n)) for i in range(n)]


def _compress_weights(pos, w1, w2):
    half = NSA_CMP_LEN // 2
    eye = jnp.eye(NSA_KV_HEADS, dtype=F32)
    pos_g = jnp.broadcast_to(pos[:, :, None, :], (2, NSA_CMP_LEN, NSA_KV_HEADS, HEAD_DIM))
    pos_a = pos_g[:, :half].reshape(2, 1, -1)
    pos_b = pos_g[:, half:].reshape(2, 1, -1)
    w1r = w1.reshape(2, NSA_CMP_LEN, HEAD_DIM, NSA_CMP_HIDDEN)
    wide = jnp.einsum('tjdo,gh->tjgdho', w1r, eye)
    cw = half * NSA_KV_HEADS * HEAD_DIM
    w1a = wide[:, :half].reshape(2, cw, NSA_KV_HEADS * NSA_CMP_HIDDEN).astype(BF16)
    w1b = wide[:, half:].reshape(2, cw, NSA_KV_HEADS * NSA_CMP_HIDDEN).astype(BF16)
    w2bd = jnp.einsum('tod,gh->tgohd', w2, eye).reshape(2, NSA_KV_HEADS * NSA_CMP_HIDDEN,
                                                        NSA_KV_HEADS * HEAD_DIM).astype(BF16)
    return pos_a, pos_b, w1a, w1b, w2bd


def _moba_prompt(proj3, batch, n):
    pairs = MOBA_HEADS // 2
    rows_cfg, out_cfg = _pair_cfg()
    sl = _alibi(MOBA_HEADS)
    slopes = _slope_table([[sl[2 * p], sl[2 * p + 1]] for p in range(pairs)])
    kmean = moba_kmean(proj3, batch, n)
    sel = moba_gate(proj3, kmean, batch, n, tq=min(n, 512))
    q0, k0, v0 = (EVEN_OFF[i] // LANES for i in range(3))
    return flash_attention("sel", proj3, batch=batch, n=n, groups=pairs,
                           q_blk=lambda p: q0 + p, k_blk=lambda p: k0 + p, v_blk=lambda p: v0 + p,
                           rows_cfg=rows_cfg, out_cfg=out_cfg, out_width=MOBA_HEADS * HEAD_DIM, out_blk=lambda p: p,
                           tq=min(n, 512), tk=min(n, 1024), W=MOBA_BLOCK, slopes=slopes, sel=sel, name="moba_attn")


def _nsa_prompt(proj3, batch, n, pos, w1, w2):
    sl = _alibi(NSA_HEADS)
    kv = proj3[:, :, EVEN_OFF[4]:EVEN_OFF[6]]
    chunks = jnp.stack([kv[:, :, :LANES], kv[:, :, LANES:]], 0).reshape(2, batch, n // NSA_CMP_STRIDE, -1)
    cmp = nsa_compress(chunks, *_compress_weights(pos, w1, w2))
    outs = [[], [], []]
    for g in range(NSA_KV_HEADS):
        rows_cfg, out_cfg = _group_cfg(g)
        slopes = _slope_table([sl[4 * g:4 * g + 4]])
        oc, sel = nsa_cmp_select(proj3, cmp, slopes, g, batch, n, tq=min(n, 256))
        common = dict(batch=batch, n=n, groups=1, q_blk=lambda p, g=g: EVEN_OFF[3] // (2 * LANES) + g,
                      rows_cfg=rows_cfg, out_cfg=out_cfg, out_width=2 * LANES, out_blk=lambda p: 0, slopes=slopes)
        osel = flash_attention("sel", proj3, k_blk=lambda p: EVEN_OFF[6] // LANES, v_blk=lambda p: EVEN_OFF[7] // LANES,
                               tq=min(n, 256), tk=min(n, 1024), W=NSA_SEL_BLOCK, sel=sel, name="nsa_sel_attn", **common)
        owin = flash_attention("win", proj3, k_blk=lambda p: EVEN_OFF[8] // LANES, v_blk=lambda p: EVEN_OFF[9] // LANES,
                               tq=NSA_WINDOW, tk=NSA_WINDOW, name="nsa_win_attn", **common)
        for lst, o in zip(outs, (oc, osel, owin)):
            lst.append(o)
    return [jnp.concatenate(lst, -1) for lst in outs]


def _fox_prompt(proj3, cum, batch, n):
    pairs = FOX_HEADS // 2
    rows_cfg, out_cfg = _pair_cfg()
    q0, k0, v0 = (ODD_OFF[i] // LANES for i in range(3))
    cum_rows = cum.swapaxes(1, 2).reshape(batch, pairs, 2, n)
    return flash_attention("fox", proj3, batch=batch, n=n, groups=pairs,
                           q_blk=lambda p: q0 + p, k_blk=lambda p: k0 + p, v_blk=lambda p: v0 + p,
                           rows_cfg=rows_cfg, out_cfg=out_cfg, out_width=FOX_HEADS * HEAD_DIM, out_blk=lambda p: p,
                           tq=min(n, 512), tk=min(n, 1024), cum=cum_rows, name="fox_attn")


def _peer_layer(h, lw, tm):
    route = peer_route(h, lw['peer_wq'], lw['peer_sk'], tm)
    return peer_experts(h, route, lw['peer_u'], lw['peer_vt'], lw['ln_ffn_g'], lw['ln_ffn_b'], tm, na=8)


def _prompt_trunk(x, lws):
    batch, n, d = x.shape
    t = batch * n
    h = x.reshape(t, d)
    tm = 512
    moba_rows, nsa_rows, win_rows, fox_rows, logf_rows = [], [], [], [], []
    for layer, lw in enumerate(lws):
        proj = matmul(h, lw['w_in'], tm, 256)
        proj3 = proj.reshape(batch, n, -1)
        if layer % 2 == 0:
            oa = _moba_prompt(proj3, batch, n)
            oc, osel, owin = _nsa_prompt(proj3, batch, n, lw['cmp_pos'], lw['cmp_w1'], lw['cmp_w2'])
            hw = MOBA_HEADS * HEAD_DIM
            h = out_proj_even(oa.reshape(t, hw), oc.reshape(t, hw), osel.reshape(t, hw), owin.reshape(t, hw), proj, h,
                              lw['w_out'], lw['ln_mix_g'], lw['ln_mix_b'], tm=256)
            moba_rows.append(proj3[:, :, EVEN_OFF[1]:EVEN_OFF[3]].reshape(batch, n, 2, MOBA_HEADS, HEAD_DIM))
            nsa_rows.append(proj3[:, :, EVEN_OFF[4]:EVEN_OFF[8]].reshape(batch, n, 4, NSA_KV_HEADS, HEAD_DIM))
            keep = min(NSA_WINDOW, n)
            win_rows.append(proj3[:, n - keep:, EVEN_OFF[8]:EVEN_OFF[10]].reshape(batch, keep, 2, NSA_KV_HEADS, HEAD_DIM))
        else:
            logf = jax.nn.log_sigmoid(proj3[:, :, ODD_OFF[3]:ODD_OFF[4]] + lw['b_forget'])
            cum = jnp.cumsum(logf, axis=1)
            o = _fox_prompt(proj3, cum, batch, n)
            h = out_proj_odd(o.reshape(t, d), h, lw['w_out'], lw['ln_mix_g'], lw['ln_mix_b'], tm=256)
            fox_rows.append(proj3[:, :, ODD_OFF[1]:ODD_OFF[3]].reshape(batch, n, 2, FOX_HEADS, HEAD_DIM))
            logf_rows.append(logf)
        h = _peer_layer(h, lw, tm)
    return (h.reshape(batch, n, d), jnp.stack(moba_rows, 1), jnp.stack(nsa_rows, 1), jnp.stack(win_rows, 1),
            jnp.stack(fox_rows, 1), jnp.stack(logf_rows, 1))


def _bf(x):
    return x.astype(BF16).astype(F32)


def _head_lanes(h):
    return slice(h * HEAD_DIM, (h + 1) * HEAD_DIM)


def _flat_pages(cache):
    return cache.reshape(cache.shape[0], cache.shape[1], cache.shape[2], -1)


def _page_head(kv_ref, c, h, ncomp, nheads):
    del ncomp
    return kv_ref[0, 0, :, (c * nheads + h) * HEAD_DIM:(c * nheads + h + 1) * HEAD_DIM]


def _tree_sum(parts):
    while len(parts) > 1:
        parts = [parts[i] + parts[i + 1] if i + 1 < len(parts) else parts[i] for i in range(0, len(parts), 2)]
    return parts[0]


def _pair_rows(row, nheads, masked):
    rid = lax.broadcasted_iota(jnp.int32, (nheads, LANES), 0)
    lane = lax.broadcasted_iota(jnp.int32, (nheads, LANES), 1)
    out = jnp.zeros((nheads, LANES), F32)
    for p in range(nheads // 2):
        out = jnp.where(rid // 2 == p, row[:, p * LANES:(p + 1) * LANES], out)
    if masked:
        out = jnp.where((lane >= HEAD_DIM) == (rid % 2 == 1), out, 0.0)
    return out


def _pair_block(kv_ref, c, p, nheads):
    start = (c * nheads + 2 * p) * HEAD_DIM
    return kv_ref[0, 0, :, start:start + LANES]


def _pick_pairs(parts, rid):
    out = parts[0]
    for p in range(1, len(parts)):
        out = jnp.where(rid // 2 == p, parts[p], out)
    return out


def _store_head_halves(o_ref, mat):
    for h in range(mat.shape[0]):
        half = (h % 2) * HEAD_DIM
        o_ref[0, :, _head_lanes(h)] = mat[h:h + 1, half:half + HEAD_DIM]


def _rows_from_lanes(row, nheads):
    rid = lax.broadcasted_iota(jnp.int32, (nheads, HEAD_DIM), 0)
    out = jnp.zeros((nheads, HEAD_DIM), F32)
    for h in range(nheads):
        out = jnp.where(rid == h, row[:, _head_lanes(h)], out)
    return out


def _store_head_rows(o_ref, mat):
    for h in range(mat.shape[0]):
        o_ref[0, :, _head_lanes(h)] = mat[h:h + 1, :]


def _own_token_update(m_prev, l_prev, acc_prev, qm, knm, vnm):
    s_own = jnp.sum(_bf(knm) * _bf(qm), axis=1, keepdims=True)
    m_new = jnp.maximum(m_prev, s_own)
    alpha = jnp.exp(m_prev - m_new)
    p_own = jnp.exp(s_own - m_new)
    return alpha * l_prev + p_own, alpha * acc_prev + _bf(p_own) * _bf(vnm)


def _fox_dec_kernel(pt_ref, q_ref, kn_ref, vn_ref, lfn_ref, kv_ref, lf_ref, o_ref, qs_scr, m_scr, l_scr, acc_scr,
                    suf_scr, *, npages):
    nh = FOX_HEADS
    j = pl.program_id(1)
    rid = lax.broadcasted_iota(jnp.int32, (nh, LANES), 0)
    eye = (lax.broadcasted_iota(jnp.int32, (nh, nh), 0) == lax.broadcasted_iota(jnp.int32, (nh, nh), 1))

    @pl.when(j == 0)
    def _init():
        qw = _pair_rows(q_ref[0], nh, True) * SCALE
        for p in range(nh // 2):
            qs_scr[p] = jnp.where(rid // 2 == p, qw, 0.0).astype(BF16)
        m_scr[...] = jnp.full(m_scr.shape, NEG, F32)
        l_scr[...] = jnp.zeros(l_scr.shape, F32)
        acc_scr[...] = jnp.zeros(acc_scr.shape, F32)
        suf_scr[...] = jnp.sum(jnp.where(eye, lfn_ref[0], 0.0), axis=1, keepdims=True)

    logf = lf_ref[0, 0]
    nrow = logf.shape[0]
    later = (lax.broadcasted_iota(jnp.int32, (nrow, nrow), 0)
             > lax.broadcasted_iota(jnp.int32, (nrow, nrow), 1)).astype(BF16)
    hi = logf.astype(BF16)
    r1 = logf - hi.astype(F32)
    mid = r1.astype(BF16)
    lo = (r1 - mid.astype(F32)).astype(BF16)
    eye_b = eye.astype(BF16)
    pieces_t = [_nt_dot(eye_b, piece) for piece in (hi, mid, lo)]
    total = _tree_sum(pieces_t)
    s = suf_scr[...] + _tree_sum([jnp.dot(pt.astype(BF16), later, preferred_element_type=F32) for pt in pieces_t])
    s = s + _tree_sum([_nt_dot(qs_scr[p], _pair_block(kv_ref, 0, p, nh).astype(BF16)) for p in range(nh // 2)])
    m_prev = m_scr[...]
    m_new = jnp.maximum(m_prev, jnp.max(s, axis=1, keepdims=True))
    alpha = jnp.exp(m_prev - m_new)
    p = jnp.exp(s - m_new)
    l_scr[...] = alpha * l_scr[...] + jnp.sum(p, axis=1, keepdims=True)
    pb = p.astype(BF16)
    acc = alpha * acc_scr[...]
    acc = acc + _pick_pairs([jnp.dot(pb, _pair_block(kv_ref, 1, p, nh).astype(BF16), preferred_element_type=F32)
                             for p in range(nh // 2)], rid)
    acc_scr[...] = acc
    m_scr[...] = m_new
    suf_scr[...] = suf_scr[...] + jnp.sum(total, axis=1, keepdims=True)

    @pl.when(j == npages - 1)
    def _fin():
        qw = _pair_rows(q_ref[0], nh, True) * SCALE
        l_fin, acc_fin = _own_token_update(m_scr[...], l_scr[...], acc_scr[...], qw,
                                           _pair_rows(kn_ref[0], nh, False), _pair_rows(vn_ref[0], nh, False))
        _store_head_halves(o_ref, acc_fin / l_fin)


def fox_decode(proj3, logf_new, cache_kv, cache_logf, page_table, layer):
    batch, npages = page_table.shape
    width = FOX_HEADS * HEAD_DIM
    col = lambda c: (lambda b, j, pt: (b, 0, c))
    page = lambda b, j, pt: (pt[b, npages - 1 - j], layer, 0, 0)
    grid_spec = pltpu.PrefetchScalarGridSpec(
        num_scalar_prefetch=1, grid=(batch, npages),
        in_specs=[pl.BlockSpec((1, 1, width), col(0)), pl.BlockSpec((1, 1, width), col(1)),
                  pl.BlockSpec((1, 1, width), col(2)), pl.BlockSpec((1, 1, FOX_HEADS), lambda b, j, pt: (b, 0, 0)),
                  pl.BlockSpec((1, 1, PAGE_SIZE, 2 * width), page),
                  pl.BlockSpec((1, 1, PAGE_SIZE, FOX_HEADS), lambda b, j, pt: (pt[b, npages - 1 - j], layer, 0, 0))],
        out_specs=pl.BlockSpec((1, 1, width), lambda b, j, pt: (b, 0, 0)),
        scratch_shapes=[pltpu.VMEM((FOX_HEADS // 2, FOX_HEADS, LANES), BF16),
                        pltpu.VMEM((FOX_HEADS, 1), F32), pltpu.VMEM((FOX_HEADS, 1), F32),
                        pltpu.VMEM((FOX_HEADS, LANES), F32), pltpu.VMEM((FOX_HEADS, 1), F32)])
    return pl.pallas_call(
        functools.partial(_fox_dec_kernel, npages=npages), grid_spec=grid_spec,
        out_shape=jax.ShapeDtypeStruct((batch, 1, width), F32),
        compiler_params=_cparams(("parallel", "arbitrary")), name="fox_decode")(
            page_table, proj3, proj3, proj3, logf_new, _flat_pages(cache_kv), cache_logf)


def _moba_dec_kernel(pt_ref, sl_ref, q_ref, kn_ref, vn_ref, kv_ref, o_ref, qs_scr, m_scr, l_scr, acc_scr, ks_scr, *,
                     npages):
    j = pl.program_id(1)
    blk = j // (MOBA_BLOCK // PAGE_SIZE)
    past = npages * PAGE_SIZE

    @pl.when(j == 0)
    def _init():
        m_scr[...] = jnp.full(m_scr.shape, NEG, F32)
        l_scr[...] = jnp.zeros(l_scr.shape, F32)
        acc_scr[...] = jnp.zeros(acc_scr.shape, F32)
        ks_scr[...] = jnp.zeros(ks_scr.shape, F32)

    nh = MOBA_HEADS
    rid = lax.broadcasted_iota(jnp.int32, (nh, LANES), 0)

    @pl.when(j == 0)
    def _queries():
        qw = _pair_rows(q_ref[0], nh, True) * SCALE
        for p in range(nh // 2):
            qs_scr[p] = jnp.where(rid // 2 == p, qw, 0.0).astype(BF16)

    dist = (past - (j * PAGE_SIZE + lax.broadcasted_iota(jnp.int32, (1, PAGE_SIZE), 1))).astype(F32)
    s = -sl_ref[0, :, 0:1] * dist
    ksum = jnp.zeros((nh, LANES), F32)
    parts = []
    for p in range(nh // 2):
        k_raw = _pair_block(kv_ref, 0, p, nh)
        parts.append(_nt_dot(qs_scr[p], k_raw.astype(BF16)))
        ksum = jnp.where(rid // 2 == p, jnp.sum(k_raw, axis=0, keepdims=True), ksum)
    s = s + _tree_sum(parts)
    m_prev = m_scr[blk]
    m_new = jnp.maximum(m_prev, jnp.max(s, axis=1, keepdims=True))
    alpha = jnp.exp(m_prev - m_new)
    p = jnp.exp(s - m_new)
    l_scr[blk] = alpha * l_scr[blk] + jnp.sum(p, axis=1, keepdims=True)
    pb = p.astype(BF16)
    acc = alpha * acc_scr[blk]
    acc = acc + _pick_pairs([jnp.dot(pb, _pair_block(kv_ref, 1, p, nh).astype(BF16), preferred_element_type=F32)
                             for p in range(nh // 2)], rid)
    acc_scr[blk] = acc
    m_scr[blk] = m_new
    ks_scr[blk] = ks_scr[blk] + ksum

    @pl.when(j == npages - 1)
    def _fin():
        qm = _pair_rows(q_ref[0], nh, True)
        nblk = m_scr.shape[0]
        gate = jnp.sum(_bf(ks_scr[...] * (1.0 / MOBA_BLOCK)) * _bf(qm), axis=2, keepdims=True)
        bid = lax.broadcasted_iota(jnp.int32, gate.shape, 0)
        chosen = jnp.zeros(gate.shape, jnp.bool_)
        for _ in range(MOBA_TOPK):
            top = jnp.max(gate, axis=0, keepdims=True)
            hit = bid == jnp.min(jnp.where(gate == top, bid, nblk), axis=0, keepdims=True)
            chosen = chosen | hit
            gate = jnp.where(hit, -jnp.inf, gate)
        knm = _pair_rows(kn_ref[0], nh, False)
        vnm = _pair_rows(vn_ref[0], nh, False)
        s_own = jnp.sum(_bf(knm) * _bf(qm * SCALE), axis=1, keepdims=True)
        mb = m_scr[...]
        m_all = jnp.maximum(jnp.max(jnp.where(chosen, mb, NEG), axis=0), s_own)
        w = jnp.where(chosen, jnp.exp(mb - m_all), 0.0)
        p_own = jnp.exp(s_own - m_all)
        l_all = jnp.sum(w * l_scr[...], axis=0) + p_own
        acc_all = jnp.sum(w * acc_scr[...], axis=0) + _bf(p_own) * _bf(vnm)
        _store_head_halves(o_ref, acc_all / l_all)


def moba_decode(proj3, cache_kv, page_table, layer):
    batch, npages = page_table.shape
    past = npages * PAGE_SIZE
    assert past % MOBA_BLOCK == 0 and past // MOBA_BLOCK >= MOBA_TOPK
    nblk = past // MOBA_BLOCK
    width = MOBA_HEADS * HEAD_DIM
    slopes = _slope_table([_alibi(MOBA_HEADS)])
    col = lambda c: (lambda b, j, pt: (b, 0, c))
    grid_spec = pltpu.PrefetchScalarGridSpec(
        num_scalar_prefetch=1, grid=(batch, npages),
        in_specs=[pl.BlockSpec((1, 8, LANES), lambda b, j, pt: (0, 0, 0)),
                  pl.BlockSpec((1, 1, width), col(0)), pl.BlockSpec((1, 1, width), col(1)),
                  pl.BlockSpec((1, 1, width), col(2)),
                  pl.BlockSpec((1, 1, PAGE_SIZE, 2 * width), lambda b, j, pt: (pt[b, j], layer, 0, 0))],
        out_specs=pl.BlockSpec((1, 1, width), lambda b, j, pt: (b, 0, 0)),
        scratch_shapes=[pltpu.VMEM((MOBA_HEADS // 2, MOBA_HEADS, LANES), BF16),
                        pltpu.VMEM((nblk, MOBA_HEADS, 1), F32), pltpu.VMEM((nblk, MOBA_HEADS, 1), F32),
                        pltpu.VMEM((nblk, MOBA_HEADS, LANES), F32), pltpu.VMEM((nblk, MOBA_HEADS, LANES), F32)])
    return pl.pallas_call(
        functools.partial(_moba_dec_kernel, npages=npages), grid_spec=grid_spec,
        out_shape=jax.ShapeDtypeStruct((batch, 1, width), F32),
        compiler_params=_cparams(("parallel", "arbitrary")), name="moba_decode")(
            page_table, slopes, proj3, proj3, proj3, _flat_pages(cache_kv))


def _nsa_dec_cmp_kernel(sl_ref, q_ref, kc_ref, vc_ref, oc_ref, sel_ref, *, nch):
    past = nch * NSA_CMP_STRIDE
    q = q_ref[0]
    kc = kc_ref[0, 0]
    vc = vc_ref[0, 0]
    cmp_end = lax.broadcasted_iota(jnp.int32, (nch, 1), 0) * NSA_CMP_STRIDE + (NSA_CMP_LEN - 1)
    mask = cmp_end <= past
    dist = (past - cmp_end).astype(F32)
    ci = lax.broadcasted_iota(jnp.int32, (nch, 1), 0) * NSA_CMP_STRIDE
    sj = lax.broadcasted_iota(jnp.int32, (1, LANES), 1) * NSA_SEL_BLOCK
    cover = ((ci < sj + NSA_SEL_BLOCK) & (ci + NSA_CMP_LEN > sj)).astype(F32)
    lane = lax.broadcasted_iota(jnp.int32, (1, LANES), 1)
    own = past // NSA_SEL_BLOCK
    rep = NSA_HEADS // NSA_KV_HEADS
    for g in range(NSA_KV_HEADS):
        kg = _bf(kc[:, _head_lanes(g)])
        vg = _bf(vc[:, _head_lanes(g)])
        imp = jnp.zeros((1, LANES), F32)
        for r in range(rep):
            h = g * rep + r
            slope = sl_ref[0, h:h + 1, 0:1]
            qh = _bf(q[:, _head_lanes(h)]) * SCALE
            s = jnp.where(mask, jnp.sum(kg * qh, axis=1, keepdims=True) - slope * dist, NEG)
            m = jnp.max(s, axis=0, keepdims=True)
            e = jnp.where(mask, jnp.exp(s - m), 0.0)
            p = _bf(e / jnp.maximum(jnp.sum(e, axis=0, keepdims=True), 1e-30))
            oc_ref[0, :, _head_lanes(h)] = jnp.sum(p * vg, axis=0, keepdims=True)
            imp = imp + jnp.sum(p * cover, axis=0, keepdims=True)
        forced = (lane == 0) | (lane == own - 1)
        sel_ref[0, g:g + 1, :] = _top_lanes(jnp.where(forced, jnp.inf, imp), NSA_SEL_TOPK - 1).astype(F32)


def nsa_decode_cmp(proj3, cmp):
    batch = proj3.shape[0]
    nch = cmp.shape[2]
    assert nch * NSA_CMP_STRIDE == LANES * NSA_SEL_BLOCK
    width = NSA_HEADS * HEAD_DIM
    slopes = _slope_table([_alibi(NSA_HEADS)])
    return pl.pallas_call(
        functools.partial(_nsa_dec_cmp_kernel, nch=nch), grid=(batch,),
        in_specs=[pl.BlockSpec((1, 8, LANES), lambda b: (0, 0, 0)),
                  pl.BlockSpec((1, 1, width), lambda b: (b, 0, EVEN_OFF[3] // width)),
                  pl.BlockSpec((1, 1, nch, LANES), lambda b: (0, b, 0, 0)),
                  pl.BlockSpec((1, 1, nch, LANES), lambda b: (1, b, 0, 0))],
        out_specs=[pl.BlockSpec((1, 1, width), lambda b: (b, 0, 0)),
                   pl.BlockSpec((1, NSA_KV_HEADS, LANES), lambda b: (b, 0, 0))],
        out_shape=[jax.ShapeDtypeStruct((batch, 1, width), F32),
                   jax.ShapeDtypeStruct((batch, NSA_KV_HEADS, LANES), F32)],
        compiler_params=_cparams(("parallel",)), name="nsa_decode_cmp")(slopes, proj3, cmp, cmp)


def _nsa_dec_mix_kernel(pt_ref, sl_ref, q_ref, ksn_ref, vsn_ref, kwn_ref, vwn_ref, gl_ref, oc_ref, sel_ref, kv_ref,
                        win_ref, o_ref, qs_scr, m_scr, l_scr, acc_scr, *, npages):
    j = pl.program_id(1)
    past = npages * PAGE_SIZE
    nh = NSA_HEADS
    rep = nh // NSA_KV_HEADS
    rid = lax.broadcasted_iota(jnp.int32, (nh, HEAD_DIM), 0)
    rid1 = lax.broadcasted_iota(jnp.int32, (nh, 1), 0)
    slope = sl_ref[0, :, 0:1]

    def group_rows(x0, x1):
        return jnp.where(rid < rep, x0, x1)

    @pl.when(j == 0)
    def _init():
        qm = _rows_from_lanes(q_ref[0], nh) * SCALE
        for g in range(NSA_KV_HEADS):
            qs_scr[g] = jnp.where(rid // rep == g, qm, 0.0).astype(BF16)
        m_scr[...] = jnp.full(m_scr.shape, NEG, F32)
        l_scr[...] = jnp.zeros(l_scr.shape, F32)
        acc_scr[...] = jnp.zeros(acc_scr.shape, F32)

    pos_lane = lax.broadcasted_iota(jnp.int32, (1, PAGE_SIZE), 1)
    dist = (past - (j * PAGE_SIZE + pos_lane)).astype(F32)
    lane = lax.broadcasted_iota(jnp.int32, (1, LANES), 1)
    per_page = PAGE_SIZE // NSA_SEL_BLOCK
    s = -slope * dist
    keeps = []
    for g in range(NSA_KV_HEADS):
        selg = sel_ref[0, g:g + 1, :]
        keep = jnp.zeros((1, PAGE_SIZE), F32)
        for c in range(per_page):
            flag = jnp.sum(jnp.where(lane == j * per_page + c, selg, 0.0), axis=1, keepdims=True)
            keep = jnp.where((pos_lane >= c * NSA_SEL_BLOCK) & (pos_lane < (c + 1) * NSA_SEL_BLOCK), flag, keep)
        keeps.append(keep)
        s = s + _nt_dot(qs_scr[g], _page_head(kv_ref, 2, g, 4, NSA_KV_HEADS).astype(BF16))
    s = jnp.where(jnp.where(rid1 < rep, keeps[0], keeps[1]) > 0.5, s, NEG)
    m_prev = m_scr[...]
    m_new = jnp.maximum(m_prev, jnp.max(s, axis=1, keepdims=True))
    alpha = jnp.exp(m_prev - m_new)
    p = jnp.exp(s - m_new)
    l_scr[...] = alpha * l_scr[...] + jnp.sum(p, axis=1, keepdims=True)
    pb = p.astype(BF16)
    acc = alpha * acc_scr[...]
    for g in range(NSA_KV_HEADS):
        res = jnp.dot(pb, _page_head(kv_ref, 3, g, 4, NSA_KV_HEADS).astype(BF16), preferred_element_type=F32)
        acc = jnp.where(rid // rep == g, acc + res, acc)
    acc_scr[...] = acc
    m_scr[...] = m_new

    @pl.when(j == npages - 1)
    def _fin():
        qm = _rows_from_lanes(q_ref[0], nh) * SCALE
        halves = lambda ref: group_rows(ref[0][:, _head_lanes(0)], ref[0][:, _head_lanes(1)])
        l_s, acc_s = _own_token_update(m_scr[...], l_scr[...], acc_scr[...], qm, halves(ksn_ref), halves(vsn_ref))
        o_s = acc_s / l_s
        nwin = win_ref.shape[2]
        wdist = (nwin - lax.broadcasted_iota(jnp.int32, (1, nwin), 1)).astype(F32)
        s_w = -slope * wdist
        for g in range(NSA_KV_HEADS):
            s_w = s_w + _nt_dot(qs_scr[g], _page_head(win_ref, 0, g, 2, NSA_KV_HEADS).astype(BF16))
        m_w = jnp.max(s_w, axis=1, keepdims=True)
        p_w = jnp.exp(s_w - m_w)
        l_w = jnp.sum(p_w, axis=1, keepdims=True)
        acc_w = jnp.zeros((nh, HEAD_DIM), F32)
        for g in range(NSA_KV_HEADS):
            res = jnp.dot(p_w.astype(BF16), _page_head(win_ref, 1, g, 2, NSA_KV_HEADS).astype(BF16),
                          preferred_element_type=F32)
            acc_w = jnp.where(rid // rep == g, res, acc_w)
        l_w, acc_w = _own_token_update(m_w, l_w, acc_w, qm, halves(kwn_ref), halves(vwn_ref))
        o_w = acc_w / l_w
        sig = jax.nn.sigmoid(gl_ref[0])
        gates = []
        for c in range(3):
            col = jnp.zeros((nh, 1), F32)
            for h in range(nh):
                col = jnp.where(rid1 == h, sig[:, 3 * h + c:3 * h + c + 1], col)
            gates.append(col)
        o_c = _rows_from_lanes(oc_ref[0], nh)
        _store_head_rows(o_ref, gates[0] * o_c + gates[1] * o_s + gates[2] * o_w)


def nsa_decode_mix(proj3, oc, sel, cache_kv, state_win, page_table, layer):
    batch, npages = page_table.shape
    nwin = state_win.shape[2]
    assert nwin == NSA_WINDOW and npages * PAGE_SIZE >= nwin
    width = NSA_HEADS * HEAD_DIM
    slopes = _slope_table([_alibi(NSA_HEADS)])
    lanes_blk = lambda off: pl.BlockSpec((1, 1, LANES), lambda b, j, pt: (b, 0, off // LANES))
    grid_spec = pltpu.PrefetchScalarGridSpec(
        num_scalar_prefetch=1, grid=(batch, npages),
        in_specs=[pl.BlockSpec((1, 8, LANES), lambda b, j, pt: (0, 0, 0)),
                  pl.BlockSpec((1, 1, width), lambda b, j, pt: (b, 0, EVEN_OFF[3] // width)),
                  lanes_blk(EVEN_OFF[6]), lanes_blk(EVEN_OFF[7]), lanes_blk(EVEN_OFF[8]), lanes_blk(EVEN_OFF[9]),
                  lanes_blk(EVEN_OFF[10]),
                  pl.BlockSpec((1, 1, width), lambda b, j, pt: (b, 0, 0)),
                  pl.BlockSpec((1, NSA_KV_HEADS, LANES), lambda b, j, pt: (b, 0, 0)),
                  pl.BlockSpec((1, 1, PAGE_SIZE, 4 * NSA_KV_HEADS * HEAD_DIM), lambda b, j, pt: (pt[b, j], layer, 0, 0)),
                  pl.BlockSpec((1, 1, nwin, 2 * NSA_KV_HEADS * HEAD_DIM), lambda b, j, pt: (b, layer, 0, 0))],
        out_specs=pl.BlockSpec((1, 1, width), lambda b, j, pt: (b, 0, 0)),
        scratch_shapes=[pltpu.VMEM((NSA_KV_HEADS, NSA_HEADS, HEAD_DIM), BF16),
                        pltpu.VMEM((NSA_HEADS, 1), F32), pltpu.VMEM((NSA_HEADS, 1), F32),
                        pltpu.VMEM((NSA_HEADS, HEAD_DIM), F32)])
    return pl.pallas_call(
        functools.partial(_nsa_dec_mix_kernel, npages=npages), grid_spec=grid_spec,
        out_shape=jax.ShapeDtypeStruct((batch, 1, width), F32),
        compiler_params=_cparams(("parallel", "arbitrary")), name="nsa_decode_mix")(
            page_table, slopes, proj3, proj3, proj3, proj3, proj3, proj3, oc, sel, _flat_pages(cache_kv), _flat_pages(state_win))


def _sample_trunk(x, lws, caches, page_table):
    cache_moba, cache_nsa, state_win, cache_fox, cache_logf = caches
    cache_moba, cache_nsa, cache_fox = _flat_pages(cache_moba), _flat_pages(cache_nsa), _flat_pages(cache_fox)
    batch, n, d = x.shape
    t = batch * n
    h = x.reshape(t, d)
    assert n == 1
    npages = page_table.shape[1]
    moba_rows, nsa_rows, win_rows, fox_rows, logf_rows = [], [], [], [], []
    for layer, lw in enumerate(lws):
        i = layer // 2
        proj = matmul(h, lw['w_in'], t, 256)
        p3 = proj.reshape(batch, n, -1)
        if layer % 2 == 0:
            o_a = moba_decode(p3, cache_moba, page_table, i)
            kvc = cache_nsa[page_table, i][:, :, :, :2 * LANES]
            chunks = kvc.reshape(batch, npages * PAGE_SIZE, 2, LANES).transpose(2, 0, 1, 3).reshape(
                2, batch, npages * PAGE_SIZE // NSA_CMP_STRIDE, -1)
            cmp = nsa_compress(chunks, *_compress_weights(lw['cmp_pos'], lw['cmp_w1'], lw['cmp_w2']))
            o_c, sel = nsa_decode_cmp(p3, cmp)
            o_b = nsa_decode_mix(p3, o_c, sel, cache_nsa, state_win, page_table, i)
            mixed = jnp.concatenate([o_a.reshape(t, -1), o_b.reshape(t, -1)], -1)
            h = out_proj_odd(mixed, h, lw['w_out'], lw['ln_mix_g'], lw['ln_mix_b'], tm=t)
            moba_rows.append(p3[:, :, EVEN_OFF[1]:EVEN_OFF[3]].reshape(batch, n, 2, MOBA_HEADS, HEAD_DIM))
            nsa_rows.append(p3[:, :, EVEN_OFF[4]:EVEN_OFF[8]].reshape(batch, n, 4, NSA_KV_HEADS, HEAD_DIM))
            new_win = p3[:, :, EVEN_OFF[8]:EVEN_OFF[10]].reshape(batch, n, 2, NSA_KV_HEADS, HEAD_DIM)
            win_rows.append(jnp.concatenate([state_win[:, i, n:], new_win], 1))
        else:
            logf = jax.nn.log_sigmoid(p3[:, :, ODD_OFF[3]:ODD_OFF[4]] + lw['b_forget'])
            o_c = fox_decode(p3, logf, cache_fox, cache_logf, page_table, i)
            h = out_proj_odd(o_c.reshape(t, -1), h, lw['w_out'], lw['ln_mix_g'], lw['ln_mix_b'], tm=t)
            fox_rows.append(p3[:, :, ODD_OFF[1]:ODD_OFF[3]].reshape(batch, n, 2, FOX_HEADS, HEAD_DIM))
            logf_rows.append(logf)
        h = _peer_layer(h, lw, t)
    return (h.reshape(batch, n, d), jnp.stack(moba_rows, 1), jnp.stack(nsa_rows, 1), jnp.stack(win_rows, 1),
            jnp.stack(fox_rows, 1), jnp.stack(logf_rows, 1))


def _layer_weights(layer, w_in_ab, w_out_ab, nsa_cmp_pos, nsa_cmp_w1, nsa_cmp_w2, w_in_c, b_forget, w_out_c,
                   ln_mix_g, ln_mix_b, ln_ffn_g, ln_ffn_b, peer_wq, peer_subkeys, peer_u, peer_v):
    i = layer // 2
    lw = dict(ln_mix_g=ln_mix_g[layer][None], ln_mix_b=ln_mix_b[layer][None],
              ln_ffn_g=ln_ffn_g[layer][None], ln_ffn_b=ln_ffn_b[layer][None],
              peer_wq=peer_wq[layer].astype(BF16), peer_sk=peer_subkeys[layer].astype(BF16),
              peer_u=peer_u[layer].astype(BF16), peer_vt=peer_v[layer].astype(BF16).T)
    if layer % 2 == 0:
        lw.update(w_in=_pad_cols(w_in_ab[i], 256).astype(BF16), w_out=w_out_ab[i].astype(BF16),
                  cmp_pos=nsa_cmp_pos[i], cmp_w1=nsa_cmp_w1[i], cmp_w2=nsa_cmp_w2[i])
    else:
        lw.update(w_in=_pad_cols(w_in_c[i], 256).astype(BF16), w_out=w_out_c[i].astype(BF16), b_forget=b_forget[i])
    return lw


def kernel(x_prompt, x_sample, cache_moba_kv, cache_nsa_kv, state_nsa_win, cache_fox_kv, cache_fox_logf, page_table,
           w_in_ab, w_out_ab, nsa_cmp_pos, nsa_cmp_w1, nsa_cmp_w2, w_in_c, b_forget, w_out_c,
           ln_mix_g, ln_mix_b, ln_ffn_g, ln_ffn_b, peer_wq, peer_subkeys, peer_u, peer_v):
    lws = [_layer_weights(layer, w_in_ab, w_out_ab, nsa_cmp_pos, nsa_cmp_w1, nsa_cmp_w2, w_in_c, b_forget, w_out_c,
                          ln_mix_g, ln_mix_b, ln_ffn_g, ln_ffn_b, peer_wq, peer_subkeys, peer_u, peer_v)
           for layer in range(DEPTH)]
    y_p, moba_p, nsa_p, win_p, fox_p, logf_p = _prompt_trunk(x_prompt, lws)
    y_s, moba_s, nsa_s, win_s, fox_s, logf_s = _sample_trunk(
        x_sample, lws, (cache_moba_kv, cache_nsa_kv, state_nsa_win, cache_fox_kv, cache_fox_logf), page_table)
    return (y_p, y_s, moba_p, moba_s, nsa_p, nsa_s, win_p, win_s, fox_p, fox_s, logf_p, logf_s)
```
